```python
import math
import functools
import jax
import jax.numpy as jnp
from jax import lax
import numpy as np

D_MODEL = 1024
BATCH = 8
SEQ = 2048
DEPTH = 2
DEC_BATCH = 32
DEC_SEQ = 8
PAST_LEN = 8192
PAGE_SIZE = 128

A_CHUNK = 128
D_A = D_MODEL // 2
A_GROUPS = 8
A_GROUP_DIM = D_A // A_GROUPS
D_B = D_MODEL
SSM_HEAD_DIM = 64
SSM_HEADS = D_B // SSM_HEAD_DIM
SSM_GROUPS = 4
SSM_STATE = 128
SSM_CONV = 4
SSM_CHUNK = 128
CONV_DIM = D_B + 2 * SSM_GROUPS * SSM_STATE
ATT_HEADS = 8
ATT_HEAD_DIM = 64
D_C = ATT_HEADS * ATT_HEAD_DIM
MOBA_BLOCK = 256
MOBA_TOPK = 3
Q_CHUNK = 128
ROPE_THETA = 10000.0
N_BRANCH = 3
D_FF = 4 * D_MODEL
EPS = 1e-6
IN_SIZES = (D_A, D_A, D_B, CONV_DIM, SSM_HEADS, D_C, D_C, D_C, N_BRANCH * D_MODEL)
IN_COLS = 2 * D_A + D_B + CONV_DIM + SSM_HEADS + 3 * D_C + N_BRANCH * D_MODEL

kernel_name = 'hybrid_gmlp_ssd_moba_step'


def rmsnorm(x, g):
    xf = x.astype(jnp.float32)
    y = xf * lax.rsqrt(jnp.mean(xf * xf, axis=-1, keepdims=True) + EPS)
    return (y * g.astype(jnp.float32)).astype(x.dtype)


def layernorm(x, g, b):
    xf = x.astype(jnp.float32)
    xc = xf - jnp.mean(xf, axis=-1, keepdims=True)
    y = xc * lax.rsqrt(jnp.mean(xc * xc, axis=-1, keepdims=True) + EPS)
    return (y * g.astype(jnp.float32) + b.astype(jnp.float32)).astype(x.dtype)


def split_columns(proj):
    bounds = []
    off = 0
    for size in IN_SIZES[:-1]:
        off += size
        bounds.append(off)
    return jnp.split(proj, bounds, axis=-1)


def rope(x, pos):
    dh = x.shape[-1]
    half = dh // 2
    inv = jnp.power(jnp.float32(ROPE_THETA), -jnp.arange(0, dh, 2, dtype=jnp.float32) / dh)
    ang = pos.astype(jnp.float32)[:, None] * inv[None, :]
    cos = jnp.cos(ang)[None, :, None, :]
    sin = jnp.sin(ang)[None, :, None, :]
    xf = x.astype(jnp.float32)
    x1, x2 = xf[..., :half], xf[..., half:]
    return jnp.concatenate([x1 * cos - x2 * sin, x2 * cos + x1 * sin], axis=-1).astype(x.dtype)


def spatial_gate(u, vn, w_s, b_s):
    bt, length, _ = u.shape
    lc = min(length, A_CHUNK)
    nc = length // lc
    mask = jnp.tril(jnp.ones((lc, lc), dtype=bool))
    w = jnp.where(mask[None], w_s[:, :lc, :lc], 0)
    vg = vn.reshape(bt, nc, lc, A_GROUPS, A_GROUP_DIM)
    mixed = jnp.einsum('gts,bcsge->bctge', w, vg) + jnp.swapaxes(b_s[:, :lc], 0, 1)[None, None, :, :, None]
    return u * mixed.reshape(bt, length, D_A)


def causal_dwconv(xpad, w, b):
    width = w.shape[0]
    length = xpad.shape[1] - width + 1
    out = b
    for tap in range(width):
        out = out + xpad[:, tap:tap + length] * w[tap]
    return out


def ssd_scan(x, dt, a_neg, b_in, c_in, h0):
    f32 = jnp.float32
    bt, length, nh, hp = x.shape
    ng, ns = b_in.shape[2], b_in.shape[3]
    nr = nh // ng
    q = min(length, SSM_CHUNK)
    nc = length // q
    xg = x.astype(f32).reshape(bt, nc, q, ng, nr, hp)
    dtg = dt.astype(f32).reshape(bt, nc, q, ng, nr)
    bg = b_in.astype(f32).reshape(bt, nc, q, ng, ns)
    cg = c_in.astype(f32).reshape(bt, nc, q, ng, ns)
    acs = jnp.cumsum(dtg * a_neg.astype(f32).reshape(ng, nr), axis=2)
    causal = jnp.tril(jnp.ones((q, q), dtype=bool))
    seg = acs[:, :, :, None] - acs[:, :, None, :]
    decay = jnp.exp(jnp.where(causal[:, :, None, None], seg, -jnp.inf))
    cb = jnp.einsum('bctgn,bcsgn->bctsg', cg, bg)
    w_ts = cb[..., None] * decay * dtg[:, :, None]
    y = jnp.einsum('bctsgr,bcsgrp->bctgrp', w_ts, xg)
    end_w = jnp.exp(acs[:, :, -1:] - acs) * dtg
    states = jnp.einsum('bcsgr,bcsgn,bcsgrp->bcgrpn', end_w, bg, xg)
    chunk_decay = jnp.exp(acs[:, :, -1])

    def step(h, inp):
        dec, st = inp
        return dec[..., None, None] * h + st, h

    h_last, h_in = lax.scan(step, h0.astype(f32).reshape(bt, ng, nr, hp, ns),
                            (jnp.moveaxis(chunk_decay, 1, 0), jnp.moveaxis(states, 1, 0)))
    h_in = jnp.moveaxis(h_in, 0, 1)
    y = y + jnp.einsum('bctgn,bcgrpn->bctgrp', cg, h_in) * jnp.exp(acs)[..., None]
    return y.reshape(bt, length, nh, hp), h_last.reshape(bt, nh, hp, ns)


def mamba_branch(z, xbc_raw, dt_raw, conv_prev, h0, conv_w, conv_b, dt_bias, a_log, d_skip, norm_g):
    bt, length, _ = z.shape
    xpad = jnp.concatenate([conv_prev.astype(xbc_raw.dtype), xbc_raw], axis=1)
    xbc = jax.nn.silu(causal_dwconv(xpad, conv_w, conv_b))
    xs, bm, cm = jnp.split(xbc, [D_B, D_B + SSM_GROUPS * SSM_STATE], axis=-1)
    dt = jax.nn.softplus(dt_raw.astype(jnp.float32) + dt_bias.astype(jnp.float32))
    a_neg = -jnp.exp(a_log.astype(jnp.float32))
    xh = xs.reshape(bt, length, SSM_HEADS, SSM_HEAD_DIM)
    y, h_last = ssd_scan(xh, dt, a_neg,
                         bm.reshape(bt, length, SSM_GROUPS, SSM_STATE),
                         cm.reshape(bt, length, SSM_GROUPS, SSM_STATE), h0)
    y = y + d_skip.astype(jnp.float32)[:, None] * xh.astype(jnp.float32)
    y = y.reshape(bt, length, D_B).astype(z.dtype)
    y = rmsnorm(y * jax.nn.silu(z), norm_g)
    return y, xpad[:, -(SSM_CONV - 1):], h_last.astype(h0.dtype)


def moba_prompt(q, k, v):
    f32 = jnp.float32
    bt, s_len, nh, dh = q.shape
    scale = dh ** -0.5
    nblk = -(-s_len // MOBA_BLOCK)
    pad = nblk * MOBA_BLOCK - s_len
    kp = jnp.pad(k, ((0, 0), (0, pad), (0, 0), (0, 0)))
    vp = jnp.pad(v, ((0, 0), (0, pad), (0, 0), (0, 0)))
    kbh = kp.reshape(bt, nblk, MOBA_BLOCK, nh, dh).transpose(0, 3, 1, 2, 4)
    vbh = vp.reshape(bt, nblk, MOBA_BLOCK, nh, dh).transpose(0, 3, 1, 2, 4)
    kmean = jnp.mean(kbh.astype(f32), axis=3)
    gate = jnp.einsum('bshd,bhjd->bshj', q.astype(f32), kmean)
    qblk = jnp.arange(s_len) // MOBA_BLOCK
    past = jnp.arange(nblk)[None, :] < qblk[:, None]
    gate = jnp.where(past[None, :, None, :], gate, -jnp.inf)
    n_sel = max(1, min(MOBA_TOPK, nblk - 1))
    _, sel = lax.top_k(gate, n_sel)
    n_qc = s_len // Q_CHUNK
    q_items = q.reshape(bt * n_qc, Q_CHUNK, nh, dh)
    sel_items = sel.reshape(bt * n_qc, Q_CHUNK, nh, n_sel)
    item_b = jnp.repeat(jnp.arange(bt), n_qc)
    item_c = jnp.tile(jnp.arange(n_qc), bt)
    heads = jnp.arange(nh)[None, :, None]

    def one(args):
        qc, selc, b, c = args
        kb_b = kbh[b]
        vb_b = vbh[b]
        qpos = c * Q_CHUNK + jnp.arange(Q_CHUNK)
        k_sel = kb_b[heads, selc]
        v_sel = vb_b[heads, selc]
        s_sel = jnp.einsum('thd,thrkd->thrk', qc, k_sel, preferred_element_type=f32) * scale
        valid = jnp.arange(n_sel)[None, :] < (qpos // MOBA_BLOCK)[:, None]
        s_sel = jnp.where(valid[:, None, :, None], s_sel, -jnp.inf)
        ob = (c * Q_CHUNK) // MOBA_BLOCK
        k_own = lax.dynamic_index_in_dim(kb_b, ob, axis=1, keepdims=False)
        v_own = lax.dynamic_index_in_dim(vb_b, ob, axis=1, keepdims=False)
        s_own = jnp.einsum('thd,hkd->thk', qc, k_own, preferred_element_type=f32) * scale
        kpos = ob * MOBA_BLOCK + jnp.arange(MOBA_BLOCK)
        s_own = jnp.where((kpos[None, :] <= qpos[:, None])[:, None, :], s_own, -jnp.inf)
        n_s = n_sel * MOBA_BLOCK
        p = jax.nn.softmax(jnp.concatenate([s_sel.reshape(Q_CHUNK, nh, n_s), s_own], axis=-1), axis=-1)
        p = p.astype(v.dtype)
        p_sel = p[..., :n_s].reshape(Q_CHUNK, nh, n_sel, MOBA_BLOCK)
        return (jnp.einsum('thrk,thrkd->thd', p_sel, v_sel)
                + jnp.einsum('thk,hkd->thd', p[..., n_s:], v_own))

    out = lax.map(one, (q_items, sel_items, item_b, item_c))
    return out.reshape(bt, s_len, nh, dh)


def moba_sample(q, k, v, cache_k, cache_v, page_table, layer):
    f32 = jnp.float32
    db, t_len, nh, dh = q.shape
    scale = dh ** -0.5
    n_pages = page_table.shape[1]
    ppb = MOBA_BLOCK // PAGE_SIZE
    n_full = (n_pages * PAGE_SIZE) // MOBA_BLOCK
    own_pages = page_table[:, n_full * ppb:]
    rem = own_pages.shape[1] * PAGE_SIZE
    if rem > 0:
        k_own = jnp.concatenate([cache_k[own_pages, layer].reshape(db, rem, nh, dh).astype(k.dtype), k], axis=1)
        v_own = jnp.concatenate([cache_v[own_pages, layer].reshape(db, rem, nh, dh).astype(v.dtype), v], axis=1)
    else:
        k_own, v_own = k, v
    s_own = jnp.einsum('bthd,bkhd->bthk', q, k_own, preferred_element_type=f32) * scale
    causal = jnp.arange(rem + t_len)[None, :] <= rem + jnp.arange(t_len)[:, None]
    s_own = jnp.where(causal[None, :, None, :], s_own, -jnp.inf)
    if n_full == 0:
        p = jax.nn.softmax(s_own, axis=-1).astype(v.dtype)
        return jnp.einsum('bthk,bkhd->bthd', p, v_own)
    k_full = cache_k[page_table[:, :n_full * ppb], layer].astype(f32).reshape(db, n_full, MOBA_BLOCK, nh, dh)
    kmean = jnp.mean(k_full, axis=2)
    gate = jnp.einsum('bthd,bjhd->bthj', q.astype(f32), kmean)
    n_sel = min(MOBA_TOPK, n_full)
    _, sel = lax.top_k(gate, n_sel)
    logical = sel[..., None] * ppb + jnp.arange(ppb)
    phys = page_table[jnp.arange(db)[:, None, None, None, None], logical]
    hid = jnp.arange(nh)[None, None, :, None, None]
    n_s = n_sel * MOBA_BLOCK
    k_sel = cache_k[phys, layer, :, hid].reshape(db, t_len, nh, n_s, dh).astype(k.dtype)
    v_sel = cache_v[phys, layer, :, hid].reshape(db, t_len, nh, n_s, dh).astype(v.dtype)
    s_sel = jnp.einsum('bthd,bthkd->bthk', q, k_sel, preferred_element_type=f32) * scale
    p = jax.nn.softmax(jnp.concatenate([s_sel, s_own], axis=-1), axis=-1).astype(v.dtype)
    return (jnp.einsum('bthk,bthkd->bthd', p[..., :n_s], v_sel)
            + jnp.einsum('bthk,bkhd->bthd', p[..., n_s:], v_own))


def trunk_layer(x, pos, conv_prev, h0, attend, p):
    bt, length, _ = x.shape
    h = rmsnorm(x, p['norm1_g'])
    u, v, z, xbc_raw, dt_raw, q, k, vv, gate_logits = split_columns(h @ p['w_in'])
    vn = layernorm(jax.nn.gelu(v, approximate=False), p['ln_v_g'], p['ln_v_b'])
    y_a = spatial_gate(jax.nn.gelu(u, approximate=False), vn, p['w_spatial'], p['b_spatial'])
    y_b, conv_new, h_new = mamba_branch(z, xbc_raw, dt_raw, conv_prev, h0, p['conv_w'], p['conv_b'],
                                        p['dt_bias'], p['a_log'], p['d_skip'], p['ssm_norm_g'])
    q = rope(q.reshape(bt, length, ATT_HEADS, ATT_HEAD_DIM), pos)
    k = rope(k.reshape(bt, length, ATT_HEADS, ATT_HEAD_DIM), pos)
    vv = vv.reshape(bt, length, ATT_HEADS, ATT_HEAD_DIM)
    y_c = attend(q, k, vv).reshape(bt, length, D_C)
    gates = jax.nn.sigmoid(gate_logits + p['b_gate']).reshape(bt, length, N_BRANCH, D_MODEL)
    merged = (gates[:, :, 0] * (y_a @ p['w_a_out'])
              + gates[:, :, 1] * (y_b @ p['w_b_out'])
              + gates[:, :, 2] * (y_c @ p['w_c_out']))
    x = x + merged @ p['w_o']
    f = rmsnorm(x, p['norm2_g']) @ p['w_up']
    x = x + jnp.square(jax.nn.relu(f)) @ p['w_down']
    return x, vn, conv_new, h_new, k, vv


def setup_inputs(seed: int = 0) -> dict:
    key = jax.random.key(seed)
    ks = jax.random.split(key, 28)
    f32 = jnp.float32
    n_pages = PAST_LEN // PAGE_SIZE
    n_pool = (DEC_BATCH * n_pages * 5 + 3) // 4

    def nrm(k, shape, scale=1.0):
        return scale * jax.random.normal(k, shape, f32)

    def gain(k, shape):
        return 1.0 + 0.02 * jax.random.normal(k, shape, f32)

    dt0 = jnp.exp(jax.random.uniform(ks[17], (DEPTH, SSM_HEADS), f32, math.log(1e-3), math.log(1e-1)))
    page_table = jax.random.permutation(ks[6], n_pool)[:DEC_BATCH * n_pages]
    return {
        'x_prompt': nrm(ks[0], (BATCH, SEQ, D_MODEL)),
        'x_sample': nrm(ks[1], (DEC_BATCH, DEC_SEQ, D_MODEL)),
        'cache_k': nrm(ks[2], (n_pool, DEPTH, PAGE_SIZE, ATT_HEADS, ATT_HEAD_DIM)),
        'cache_v': nrm(ks[3], (n_pool, DEPTH, PAGE_SIZE, ATT_HEADS, ATT_HEAD_DIM)),
        'state_ssm': nrm(ks[4], (DEPTH, DEC_BATCH, SSM_HEADS, SSM_HEAD_DIM, SSM_STATE), 0.1),
        'state_conv': nrm(ks[5], (DEPTH, DEC_BATCH, SSM_CONV - 1, CONV_DIM)),
        'page_table': page_table.reshape(DEC_BATCH, n_pages).astype(jnp.int32),
        'norm1_g': gain(ks[7], (DEPTH, D_MODEL)),
        'w_in': nrm(ks[8], (DEPTH, D_MODEL, IN_COLS), D_MODEL ** -0.5),
        'b_gate': nrm(ks[9], (DEPTH, N_BRANCH * D_MODEL), 0.02),
        'ln_v_g': gain(ks[10], (DEPTH, D_A)),
        'ln_v_b': nrm(ks[11], (DEPTH, D_A), 0.02),
        'w_spatial': nrm(ks[12], (DEPTH, A_GROUPS, A_CHUNK, A_CHUNK), A_CHUNK ** -0.5),
        'b_spatial': gain(ks[13], (DEPTH, A_GROUPS, A_CHUNK)),
        'w_a_out': nrm(ks[14], (DEPTH, D_A, D_MODEL), D_A ** -0.5),
        'conv_w': nrm(ks[15], (DEPTH, SSM_CONV, CONV_DIM), SSM_CONV ** -0.5),
        'conv_b': nrm(ks[16], (DEPTH, CONV_DIM), 0.02),
        'dt_bias': dt0 + jnp.log(-jnp.expm1(-dt0)),
        'a_log': jnp.log(jax.random.uniform(ks[18], (DEPTH, SSM_HEADS), f32, 1.0, 16.0)),
        'd_skip': 1.0 + 0.1 * jax.random.normal(ks[19], (DEPTH, SSM_HEADS), f32),
        'ssm_norm_g': gain(ks[20], (DEPTH, D_B)),
        'w_b_out': nrm(ks[21], (DEPTH, D_B, D_MODEL), D_B ** -0.5),
        'w_c_out': nrm(ks[22], (DEPTH, D_C, D_MODEL), D_C ** -0.5),
        'w_o': nrm(ks[23], (DEPTH, D_MODEL, D_MODEL), D_MODEL ** -0.5),
        'norm2_g': gain(ks[24], (DEPTH, D_MODEL)),
        'w_up': nrm(ks[25], (DEPTH, D_MODEL, D_FF), D_MODEL ** -0.5),
        'w_down': nrm(ks[26], (DEPTH, D_FF, D_MODEL), D_FF ** -0.5),
        'norm_f_g': gain(ks[27], (D_MODEL,)),
    }


def reference(x_prompt, x_sample, cache_k, cache_v, state_ssm, state_conv, page_table,
              norm1_g, w_in, b_gate, ln_v_g, ln_v_b, w_spatial, b_spatial, w_a_out,
              conv_w, conv_b, dt_bias, a_log, d_skip, ssm_norm_g, w_b_out, w_c_out, w_o,
              norm2_g, w_up, w_down, norm_f_g):
    bp, s_len, _ = x_prompt.shape
    t_len = x_sample.shape[1]
    past_len = page_table.shape[1] * PAGE_SIZE
    pos_prompt = jnp.arange(s_len)
    pos_sample = past_len + jnp.arange(t_len)
    conv_zero = jnp.zeros((bp, SSM_CONV - 1, CONV_DIM), x_prompt.dtype)
    h_zero = jnp.zeros((bp, SSM_HEADS, SSM_HEAD_DIM, SSM_STATE), x_prompt.dtype)
    xp, xs = x_prompt, x_sample
    a_v_s, ssm_p, ssm_s, conv_p, conv_s, k_p, v_p, k_s, v_s = [], [], [], [], [], [], [], [], []
    for l in range(DEPTH):
        p = {
            'norm1_g': norm1_g[l], 'w_in': w_in[l], 'b_gate': b_gate[l],
            'ln_v_g': ln_v_g[l], 'ln_v_b': ln_v_b[l], 'w_spatial': w_spatial[l], 'b_spatial': b_spatial[l],
            'w_a_out': w_a_out[l], 'conv_w': conv_w[l], 'conv_b': conv_b[l], 'dt_bias': dt_bias[l],
            'a_log': a_log[l], 'd_skip': d_skip[l], 'ssm_norm_g': ssm_norm_g[l], 'w_b_out': w_b_out[l],
            'w_c_out': w_c_out[l], 'w_o': w_o[l], 'norm2_g': norm2_g[l], 'w_up': w_up[l], 'w_down': w_down[l],
        }
        xp, _, cp, hp, kp, vp = trunk_layer(xp, pos_prompt, conv_zero, h_zero, moba_prompt, p)
        attend_sample = functools.partial(moba_sample, cache_k=cache_k, cache_v=cache_v,
                                          page_table=page_table, layer=l)
        xs, vns, cs, hs, ks_, vs_ = trunk_layer(xs, pos_sample, state_conv[l], state_ssm[l], attend_sample, p)
        a_v_s.append(vns)
        ssm_p.append(hp)
        ssm_s.append(hs)
        conv_p.append(cp)
        conv_s.append(cs)
        k_p.append(kp)
        v_p.append(vp)
        k_s.append(ks_)
        v_s.append(vs_)
    y_prompt = rmsnorm(xp, norm_f_g)
    y_sample = rmsnorm(xs, norm_f_g)
    new_a_v_sample = jnp.stack(a_v_s)
    new_ssm_prompt = jnp.stack(ssm_p)
    new_ssm_sample = jnp.stack(ssm_s)
    new_conv_prompt = jnp.stack(conv_p)
    new_conv_sample = jnp.stack(conv_s)
    new_k_prompt = jnp.stack(k_p)
    new_v_prompt = jnp.stack(v_p)
    new_k_sample = jnp.stack(k_s)
    new_v_sample = jnp.stack(v_s)
    return (y_prompt, y_sample, new_a_v_sample, new_ssm_prompt, new_ssm_sample, new_conv_prompt,
            new_conv_sample, new_k_prompt, new_v_prompt, new_k_sample, new_v_sample)
```

```python
import functools
import math

import jax
import jax.numpy as jnp
from jax import lax
from jax.experimental import pallas as pl
from jax.experimental.pallas import tpu as pltpu

F32 = jnp.float32
BF16 = jnp.bfloat16
NEG_INF = float("-inf")

D_MODEL = 1024
PAGE_SIZE = 128
A_CHUNK = 128
D_A = D_MODEL // 2
A_GROUPS = 8
A_GROUP_DIM = D_A // A_GROUPS
D_B = D_MODEL
SSM_HEAD_DIM = 64
SSM_HEADS = D_B // SSM_HEAD_DIM
SSM_GROUPS = 4
SSM_STATE = 128
SSM_CONV = 4
SSM_CHUNK = 128
CONV_DIM = D_B + 2 * SSM_GROUPS * SSM_STATE
ATT_HEADS = 8
ATT_HEAD_DIM = 64
D_C = ATT_HEADS * ATT_HEAD_DIM
MOBA_BLOCK = 256
MOBA_TOPK = 3
ROPE_THETA = 10000.0
N_BRANCH = 3
D_FF = 4 * D_MODEL
EPS = 1e-6

LANES = 128
C_U, C_V, C_Z, C_XBC, C_GATE, C_Q, C_K, C_VATT, C_DT = 0, 512, 1024, 2048, 4096, 7168, 7680, 8192, 8704
DT_W = 256
N_COLS = C_DT + DT_W
IN_TN = 1280
VMEM_LIMIT = 56 * 1024 * 1024


def _cparams(sem):
    return pltpu.CompilerParams(dimension_semantics=sem, vmem_limit_bytes=VMEM_LIMIT)


def _dot(a, b):
    return jnp.dot(a, b, preferred_element_type=F32)


def _dot_nt(a, b):
    return lax.dot_general(a, b, (((1,), (1,)), ((), ())), preferred_element_type=F32)


def _split3(x):
    x1 = x.astype(BF16)
    r = x - x1.astype(F32)
    x2 = r.astype(BF16)
    r = r - x2.astype(F32)
    return x1, x2, r.astype(BF16)


def _dot_hi_nt(a, b):
    a1, a2, a3 = _split3(a)
    b1, b2, b3 = _split3(b)
    return (_dot_nt(a1, b1) + (_dot_nt(a1, b2) + _dot_nt(a2, b1))
            + (_dot_nt(a1, b3) + _dot_nt(a2, b2) + _dot_nt(a3, b1)))


def _rms(x, g):
    return x * lax.rsqrt(jnp.mean(x * x, axis=-1, keepdims=True) + EPS) * g


def _gelu(x):
    return 0.5 * x * (1.0 + lax.erf(x * math.sqrt(0.5)))


def _silu(x):
    return x * jax.nn.sigmoid(x)


def _inproj_kernel(x_ref, g_ref, w_ref, o_ref, h_ref):
    @pl.when(pl.program_id(1) == 0)
    def _():
        h_ref[...] = _rms(x_ref[...], g_ref[...]).astype(BF16)

    o_ref[...] = _dot(h_ref[...], w_ref[...])


def _in_proj(x2d, g, w_bf, tm):
    m = x2d.shape[0]
    return pl.pallas_call(
        _inproj_kernel,
        grid=(m // tm, N_COLS // IN_TN),
        in_specs=[pl.BlockSpec((tm, D_MODEL), lambda i, j: (i, 0)),
                  pl.BlockSpec((1, D_MODEL), lambda i, j: (0, 0)),
                  pl.BlockSpec((D_MODEL, IN_TN), lambda i, j: (0, j))],
        out_specs=pl.BlockSpec((tm, IN_TN), lambda i, j: (i, j)),
        out_shape=jax.ShapeDtypeStruct((m, N_COLS), F32),
        scratch_shapes=[pltpu.VMEM((tm, D_MODEL), BF16)],
        compiler_params=_cparams(("parallel", "arbitrary")),
        name="in_proj",
    )(x2d, g, w_bf)


def _layernorm(x, g, b):
    xc = x - jnp.mean(x, axis=-1, keepdims=True)
    return xc * lax.rsqrt(jnp.mean(xc * xc, axis=-1, keepdims=True) + EPS) * g + b


def _branch_a_kernel(u_ref, v_ref, g_ref, b_ref, w_ref, bs_ref, ya_ref):
    gu = _gelu(u_ref[...])
    vn = _layernorm(_gelu(v_ref[...]), g_ref[...], b_ref[...]).astype(BF16)
    first = lax.broadcasted_iota(jnp.int32, (A_CHUNK, LANES), 1) < A_GROUP_DIM
    parts = []
    for p in range(A_GROUPS // 2):
        vp = vn[:, p * LANES:(p + 1) * LANES]
        parts.append(jnp.where(first, _dot(w_ref[2 * p], vp), _dot(w_ref[2 * p + 1], vp)))
    mixed = jnp.concatenate(parts, axis=1) + bs_ref[...]
    ya_ref[...] = (gu * mixed).astype(BF16)


def _branch_a_prompt(proj, ln_g, ln_b, w_tril_bf, bs_exp):
    m = proj.shape[0]
    return pl.pallas_call(
        _branch_a_kernel,
        grid=(m // A_CHUNK,),
        in_specs=[pl.BlockSpec((A_CHUNK, D_A), lambda i: (i, C_U // D_A)),
                  pl.BlockSpec((A_CHUNK, D_A), lambda i: (i, C_V // D_A)),
                  pl.BlockSpec((1, D_A), lambda i: (0, 0)),
                  pl.BlockSpec((1, D_A), lambda i: (0, 0)),
                  pl.BlockSpec((A_GROUPS, A_CHUNK, A_CHUNK), lambda i: (0, 0, 0)),
                  pl.BlockSpec((A_CHUNK, D_A), lambda i: (0, 0))],
        out_specs=pl.BlockSpec((A_CHUNK, D_A), lambda i: (i, 0)),
        out_shape=jax.ShapeDtypeStruct((m, D_A), BF16),
        compiler_params=_cparams(("parallel",)),
        name="branch_a_prompt",
    )(proj, proj, ln_g, ln_b, w_tril_bf, bs_exp)


def _branch_a_sample_kernel(u_ref, v_ref, g_ref, b_ref, wexp_ref, bs_ref, ya_ref, vn_ref):
    gu = _gelu(u_ref[...])
    vn = _layernorm(_gelu(v_ref[...]), g_ref[...], b_ref[...])
    vn_ref[...] = vn
    mixed = bs_ref[...]
    for s in range(vn.shape[0]):
        mixed = mixed + wexp_ref[s] * vn[s:s + 1, :]
    ya_ref[...] = (gu * mixed).astype(BF16)


def _branch_a_sample(proj, t_len, ln_g, ln_b, w_exp, bs_exp):
    m = proj.shape[0]
    return pl.pallas_call(
        _branch_a_sample_kernel,
        grid=(m // t_len,),
        in_specs=[pl.BlockSpec((t_len, D_A), lambda i: (i, C_U // D_A)),
                  pl.BlockSpec((t_len, D_A), lambda i: (i, C_V // D_A)),
                  pl.BlockSpec((1, D_A), lambda i: (0, 0)),
                  pl.BlockSpec((1, D_A), lambda i: (0, 0)),
                  pl.BlockSpec((t_len, t_len, D_A), lambda i: (0, 0, 0)),
                  pl.BlockSpec((t_len, D_A), lambda i: (0, 0))],
        out_specs=[pl.BlockSpec((t_len, D_A), lambda i: (i, 0)),
                   pl.BlockSpec((t_len, D_A), lambda i: (i, 0))],
        out_shape=[jax.ShapeDtypeStruct((m, D_A), BF16), jax.ShapeDtypeStruct((m, D_A), F32)],
        compiler_params=_cparams(("parallel",)),
        name="branch_a_sample",
    )(proj, proj, ln_g, ln_b, w_exp, bs_exp)


XP_OFF = 8


def _ssd_kernel(z_ref, xbc_ref, dt_ref, cprev_ref, h0_ref, cw_ref, cb_ref, dtb_ref, alog_ref, dsk_ref, ng_ref,
                yb_ref, hout_ref, xp_ref, *, rows):
    t_len = SSM_CHUNK
    c = pl.program_id(1)
    xbc, z, dtraw = xbc_ref[...], z_ref[...], dt_ref[...][:, :LANES]
    if rows < t_len:
        def pad(a):
            return jnp.concatenate([a, jnp.zeros((t_len - rows, a.shape[1]), a.dtype)], axis=0)
        xbc, z, dtraw = pad(xbc), pad(z), pad(dtraw)

    @pl.when(c == 0)
    def _():
        xp_ref[XP_OFF - 3:XP_OFF, :] = cprev_ref[...]
        hout_ref[...] = h0_ref[...]

    @pl.when(c > 0)
    def _():
        xp_ref[XP_OFF - 3:XP_OFF, :] = xp_ref[XP_OFF + t_len - 3:XP_OFF + t_len, :]

    xp_ref[XP_OFF:XP_OFF + t_len, :] = xbc
    acc = cb_ref[...] + xbc * cw_ref[SSM_CONV - 1:SSM_CONV, :]
    for tap in range(SSM_CONV - 1):
        lo = XP_OFF - (SSM_CONV - 1) + tap
        acc = acc + xp_ref[lo:lo + t_len, :] * cw_ref[tap:tap + 1, :]
    act = _silu(acc)
    xs = act[:, :D_B]
    bm = act[:, D_B:D_B + SSM_GROUPS * SSM_STATE].astype(BF16)
    cm = act[:, D_B + SSM_GROUPS * SSM_STATE:].astype(BF16)

    dt = jax.nn.softplus(dtraw + dtb_ref[...])
    row_i = lax.broadcasted_iota(jnp.int32, (t_len, t_len), 0)
    col_i = lax.broadcasted_iota(jnp.int32, (t_len, t_len), 1)
    if rows < t_len:
        dt = jnp.where(lax.broadcasted_iota(jnp.int32, dt.shape, 0) < rows, dt, 0.0)
    a = dt * (-jnp.exp(alog_ref[...]))
    tril = jnp.where(col_i <= row_i, 1.0, 0.0).astype(BF16)
    a1, a2, a3 = _split3(a)
    acs = _dot(tril, a1) + _dot(tril, a2) + _dot(tril, a3)
    acs_t = acs.T
    dt_t = dt.T
    xs_t = xs.T
    causal_t = row_i <= col_i
    rep = SSM_HEADS // SSM_GROUPS
    ys = []
    for g in range(SSM_GROUPS):
        bg = bm[:, g * SSM_STATE:(g + 1) * SSM_STATE]
        cg = cm[:, g * SSM_STATE:(g + 1) * SSM_STATE]
        cb_t = _dot_nt(bg, cg)
        for r in range(rep):
            h = g * rep + r
            hs = slice(h * SSM_HEAD_DIM, (h + 1) * SSM_HEAD_DIM)
            acs_row = acs_t[h:h + 1, :]
            acs_col = acs[:, h:h + 1]
            dec_t = jnp.exp(jnp.where(causal_t, acs_row - acs_col, NEG_INF))
            w_t = (cb_t * dec_t).astype(BF16)
            xdt_t = xs_t[hs, :] * dt_t[h:h + 1, :]
            h_in = hout_ref[hs, :]
            y_t = _dot(xdt_t.astype(BF16), w_t) + _dot_nt(h_in.astype(BF16), cg) * jnp.exp(acs_row)
            last = acs_row[:, t_len - 1:t_len]
            st = _dot((xdt_t * jnp.exp(last - acs_row)).astype(BF16), bg)
            hout_ref[hs, :] = jnp.exp(last) * h_in + st
            ys.append(y_t)
    y = jnp.concatenate(ys, axis=0).T + dsk_ref[...] * xs
    yb = _rms(y * _silu(z), ng_ref[...])
    yb_ref[...] = yb[:rows].astype(BF16)


def _branch_b(proj, n_seq, rows, conv_prev, h0, conv_w, conv_b, dt_bias_p, a_log_p, d_skip_exp, norm_g):
    m = proj.shape[0]
    n_chunk = m // (n_seq * rows)
    const = lambda b, c: (0, 0)
    return pl.pallas_call(
        functools.partial(_ssd_kernel, rows=rows),
        grid=(n_seq, n_chunk),
        in_specs=[pl.BlockSpec((rows, D_B), lambda b, c: (b * n_chunk + c, C_Z // D_B)),
                  pl.BlockSpec((rows, CONV_DIM), lambda b, c: (b * n_chunk + c, C_XBC // CONV_DIM)),
                  pl.BlockSpec((rows, DT_W), lambda b, c: (b * n_chunk + c, C_DT // DT_W)),
                  pl.BlockSpec((None, SSM_CONV - 1, CONV_DIM), lambda b, c: (b, 0, 0)),
                  pl.BlockSpec((None, D_B, SSM_STATE), lambda b, c: (b, 0, 0)),
                  pl.BlockSpec((SSM_CONV, CONV_DIM), const),
                  pl.BlockSpec((1, CONV_DIM), const),
                  pl.BlockSpec((1, LANES), const),
                  pl.BlockSpec((1, LANES), const),
                  pl.BlockSpec((1, D_B), const),
                  pl.BlockSpec((1, D_B), const)],
        out_specs=[pl.BlockSpec((rows, D_B), lambda b, c: (b * n_chunk + c, 0)),
                   pl.BlockSpec((None, D_B, SSM_STATE), lambda b, c: (b, 0, 0))],
        out_shape=[jax.ShapeDtypeStruct((m, D_B), BF16),
                   jax.ShapeDtypeStruct((n_seq, D_B, SSM_STATE), F32)],
        scratch_shapes=[pltpu.VMEM((XP_OFF + SSM_CHUNK, CONV_DIM), F32)],
        compiler_params=_cparams(("parallel", "arbitrary")),
        name="branch_b_rows%d" % rows,
    )(proj, proj, proj, conv_prev, h0, conv_w, conv_b, dt_bias_p, a_log_p, d_skip_exp, norm_g)


def _rope_apply(x, cos, sin_signed, first):
    partner = jnp.where(first, pltpu.roll(x, D_C - ATT_HEAD_DIM // 2, 1), pltpu.roll(x, ATT_HEAD_DIM // 2, 1))
    return x * cos + partner * sin_signed


def _rope_kernel(q_ref, k_ref, cos_ref, sin_ref, qo_ref, ko_ref, *km_refs):
    lane = lax.broadcasted_iota(jnp.int32, q_ref.shape, 1)
    first = (lane % ATT_HEAD_DIM) < ATT_HEAD_DIM // 2
    cos, sin = cos_ref[...], sin_ref[...]
    qo_ref[...] = _rope_apply(q_ref[...], cos, sin, first)
    kr = _rope_apply(k_ref[...], cos, sin, first)
    ko_ref[...] = kr
    if km_refs:
        km_refs[0][pl.ds(pl.program_id(1), 1), :] = jnp.mean(kr, axis=0, keepdims=True)


def _rope_prompt(proj, n_seq, cos, sin):
    m = proj.shape[0]
    nblk = m // (n_seq * MOBA_BLOCK)
    row = lambda b, j: (b * nblk + j, 0)
    return pl.pallas_call(
        _rope_kernel,
        grid=(n_seq, nblk),
        in_specs=[pl.BlockSpec((MOBA_BLOCK, D_C), lambda b, j: (b * nblk + j, C_Q // D_C)),
                  pl.BlockSpec((MOBA_BLOCK, D_C), lambda b, j: (b * nblk + j, C_K // D_C)),
                  pl.BlockSpec((MOBA_BLOCK, D_C), lambda b, j: (j, 0)),
                  pl.BlockSpec((MOBA_BLOCK, D_C), lambda b, j: (j, 0))],
        out_specs=[pl.BlockSpec((MOBA_BLOCK, D_C), row),
                   pl.BlockSpec((MOBA_BLOCK, D_C), row),
                   pl.BlockSpec((None, nblk, D_C), lambda b, j: (b, 0, 0))],
        out_shape=[jax.ShapeDtypeStruct((m, D_C), F32), jax.ShapeDtypeStruct((m, D_C), F32),
                   jax.ShapeDtypeStruct((n_seq, nblk, D_C), F32)],
        compiler_params=_cparams(("parallel", "arbitrary")),
        name="rope_prompt",
    )(proj, proj, cos, sin)


def _rope_sample(proj, cos, sin):
    m = proj.shape[0]
    return pl.pallas_call(
        _rope_kernel,
        grid=(1,),
        in_specs=[pl.BlockSpec((m, D_C), lambda i: (0, C_Q // D_C)),
                  pl.BlockSpec((m, D_C), lambda i: (0, C_K // D_C)),
                  pl.BlockSpec((m, D_C), lambda i: (0, 0)),
                  pl.BlockSpec((m, D_C), lambda i: (0, 0))],
        out_specs=[pl.BlockSpec((m, D_C), lambda i: (0, 0)), pl.BlockSpec((m, D_C), lambda i: (0, 0))],
        out_shape=[jax.ShapeDtypeStruct((m, D_C), F32), jax.ShapeDtypeStruct((m, D_C), F32)],
        compiler_params=_cparams(("arbitrary",)),
        name="rope_sample",
    )(proj, proj, cos, sin)


def _topk_rank(gate, idx, n_cand, axis):
    rank = jnp.zeros(gate.shape, jnp.int32)
    for i in range(n_cand):
        gi = gate[:, i:i + 1] if axis == 1 else gate[i:i + 1, :]
        beats = (gi > gate) | ((gi == gate) & (i < idx))
        rank = rank + jnp.where(beats, 1, 0)
    return rank


def _moba_prompt_kernel(q_ref, k_ref, v_ref, km_ref, o_ref, *, nblk):
    qb = pl.program_id(2)
    scale = ATT_HEAD_DIM ** -0.5
    q = q_ref[...]
    kb = k_ref[...].astype(BF16)
    vb = v_ref[...].astype(BF16)
    km = km_ref[...]
    lane = lax.broadcasted_iota(jnp.int32, (MOBA_BLOCK, LANES), 1)
    blk = lax.broadcasted_iota(jnp.int32, (MOBA_BLOCK, nblk), 1)
    causal = jnp.where(lax.broadcasted_iota(jnp.int32, (MOBA_BLOCK, MOBA_BLOCK), 1)
                       <= lax.broadcasted_iota(jnp.int32, (MOBA_BLOCK, MOBA_BLOCK), 0), 1, 0)
    outs = []
    for h in range(LANES // ATT_HEAD_DIM):
        head = (lane >= h * ATT_HEAD_DIM) & (lane < (h + 1) * ATT_HEAD_DIM)
        qm = jnp.where(head, q, 0.0)
        gate = jnp.where(blk < qb, _dot_hi_nt(qm, km), NEG_INF)
        rank = _topk_rank(gate, blk, nblk - 1, 1)
        sel = jnp.where((blk < qb) & (rank < MOBA_TOPK), 1, 0)
        s = _dot_nt(qm.astype(BF16), kb) * scale
        sj, m = [], None
        for j in range(nblk):
            allowed = jnp.where(qb == j, causal, sel[:, j:j + 1]) > 0
            sb = jnp.where(allowed, s[:, j * MOBA_BLOCK:(j + 1) * MOBA_BLOCK], NEG_INF)
            sj.append(sb)
            mj = jnp.max(sb, axis=1, keepdims=True)
            m = mj if m is None else jnp.maximum(m, mj)
        l = jnp.zeros((MOBA_BLOCK, 1), F32)
        o = jnp.zeros((MOBA_BLOCK, LANES), F32)
        for j in range(nblk):
            p = jnp.exp(sj[j] - m)
            l = l + jnp.sum(p, axis=1, keepdims=True)
            o = o + _dot(p.astype(BF16), vb[j * MOBA_BLOCK:(j + 1) * MOBA_BLOCK, :])
        outs.append(o / l)
    o_ref[...] = jnp.where(lane < ATT_HEAD_DIM, outs[0], outs[1]).astype(BF16)


def _moba_prompt(q_rot, k_rot, proj, kmean, n_seq):
    m = q_rot.shape[0]
    s_len = m // n_seq
    nblk = s_len // MOBA_BLOCK
    n_pair = D_C // LANES
    return pl.pallas_call(
        functools.partial(_moba_prompt_kernel, nblk=nblk),
        grid=(n_seq, n_pair, nblk),
        in_specs=[pl.BlockSpec((MOBA_BLOCK, LANES), lambda b, hp, i: (b * nblk + i, hp)),
                  pl.BlockSpec((s_len, LANES), lambda b, hp, i: (b, hp)),
                  pl.BlockSpec((s_len, LANES), lambda b, hp, i: (b, C_VATT // LANES + hp)),
                  pl.BlockSpec((None, nblk, LANES), lambda b, hp, i: (b, 0, hp))],
        out_specs=pl.BlockSpec((MOBA_BLOCK, LANES), lambda b, hp, i: (b * nblk + i, hp)),
        out_shape=jax.ShapeDtypeStruct((m, D_C), BF16),
        compiler_params=_cparams(("parallel", "parallel", "arbitrary")),
        name="moba_prompt",
    )(q_rot, k_rot, proj, kmean)


def _moba_sample_kernel(pt_ref, q_ref, kn_ref, vn_ref, ck_ref, cv_ref, o_ref,
                        qexp_ref, st_ref, km_ref, inv_ref, oacc_ref, *, n_pages, t_len):
    del pt_ref
    p = pl.program_id(1)
    scale = ATT_HEAD_DIM ** -0.5
    ppb = MOBA_BLOCK // PAGE_SIZE
    n_blk = n_pages // ppb
    n_keys = n_pages * PAGE_SIZE
    n_col = ATT_HEADS * t_len

    @pl.when(p == 0)
    def _():
        rowh = lax.broadcasted_iota(jnp.int32, (LANES, ATT_HEAD_DIM), 0) // t_len
        for h in range(ATT_HEADS):
            tiled = jnp.concatenate([q_ref[h]] * (LANES // t_len), axis=0)
            qexp_ref[h] = jnp.where(rowh == h, tiled, 0.0)

    @pl.when(p < n_pages)
    def _():
        acc = jnp.zeros((PAGE_SIZE, LANES), F32)
        for h in range(ATT_HEADS):
            acc = acc + _dot_nt(ck_ref[:, h, :].astype(BF16), qexp_ref[h].astype(BF16))
        st_ref[pl.ds(pl.multiple_of(p * PAGE_SIZE, PAGE_SIZE), PAGE_SIZE), :] = acc * scale
        ksum = jnp.sum(ck_ref[...], axis=0)
        j = p // ppb

        @pl.when(p % ppb == 0)
        def _():
            km_ref[j] = ksum

        @pl.when(p % ppb != 0)
        def _():
            km_ref[j] = km_ref[j] + ksum

    @pl.when(p == n_pages - 1)
    def _():
        gate = jnp.zeros((n_blk, LANES), F32)
        for h in range(ATT_HEADS):
            gate = gate + _dot_hi_nt(km_ref[:, h, :] * (1.0 / MOBA_BLOCK), qexp_ref[h])
        bidx = lax.broadcasted_iota(jnp.int32, (n_blk, LANES), 0)
        sel = jnp.where(_topk_rank(gate, bidx, n_blk, 0) < min(MOBA_TOPK, n_blk), 1, 0)
        so = jnp.zeros((t_len, LANES), F32)
        for h in range(ATT_HEADS):
            so = so + _dot_nt(kn_ref[:, h, :].astype(BF16), qexp_ref[h].astype(BF16))
        key_i = lax.broadcasted_iota(jnp.int32, (t_len, LANES), 0)
        tok_i = lax.broadcasted_iota(jnp.int32, (t_len, LANES), 1) % t_len
        so = jnp.where(key_i <= tok_i, so * scale, NEG_INF)
        m = jnp.max(so, axis=0, keepdims=True)
        for j in range(n_blk):
            sb = jnp.where(sel[j:j + 1, :] > 0, st_ref[j * MOBA_BLOCK:(j + 1) * MOBA_BLOCK, :], NEG_INF)
            m = jnp.maximum(m, jnp.max(sb, axis=0, keepdims=True))
        po = jnp.exp(so - m)
        l = jnp.sum(po, axis=0, keepdims=True)
        for j in range(n_blk):
            sb = jnp.where(sel[j:j + 1, :] > 0, st_ref[j * MOBA_BLOCK:(j + 1) * MOBA_BLOCK, :], NEG_INF)
            pj = jnp.exp(sb - m)
            st_ref[j * MOBA_BLOCK:(j + 1) * MOBA_BLOCK, :] = pj
            l = l + jnp.sum(pj, axis=0, keepdims=True)
        inv_ref[...] = 1.0 / l
        st_ref[n_keys:n_keys + PAGE_SIZE, :] = jnp.concatenate(
            [po, jnp.zeros((PAGE_SIZE - t_len, LANES), F32)], axis=0)
        oacc_ref[...] = jnp.zeros(oacc_ref.shape, F32)

    def pv(p_t, v_of_head):
        pm = p_t.T.astype(BF16)
        return [_dot(pm[h * t_len:(h + 1) * t_len, :], v_of_head(h).astype(BF16)) for h in range(ATT_HEADS)]

    @pl.when(p >= n_pages)
    def _():
        start = pl.multiple_of((p - n_pages) * PAGE_SIZE, PAGE_SIZE)
        outs = pv(st_ref[pl.ds(start, PAGE_SIZE), :] * inv_ref[...], lambda h: cv_ref[:, h, :])
        for h in range(ATT_HEADS):
            oacc_ref[h] = oacc_ref[h] + outs[h]

    @pl.when(p == 2 * n_pages - 1)
    def _():
        zpad = jnp.zeros((PAGE_SIZE - t_len, ATT_HEAD_DIM), F32)
        outs = pv(st_ref[n_keys:n_keys + PAGE_SIZE, :] * inv_ref[...],
                  lambda h: jnp.concatenate([vn_ref[:, h, :], zpad], axis=0))
        for h in range(ATT_HEADS):
            o_ref[h] = oacc_ref[h] + outs[h]
    del n_col


def _moba_sample(q_htd, k_thd, v_thd, cache_k, cache_v, page_table, layer):
    n_seq, _, t_len, _ = q_htd.shape
    n_pages = page_table.shape[1]
    page_blk = (None, None, PAGE_SIZE, ATT_HEADS, ATT_HEAD_DIM)
    tok_blk = (None, ATT_HEADS, t_len, ATT_HEAD_DIM)
    new_blk = (None, t_len, ATT_HEADS, ATT_HEAD_DIM)
    grid_spec = pltpu.PrefetchScalarGridSpec(
        num_scalar_prefetch=1,
        grid=(n_seq, 2 * n_pages),
        in_specs=[pl.BlockSpec(tok_blk, lambda b, p, pt: (b, 0, 0, 0)),
                  pl.BlockSpec(new_blk, lambda b, p, pt: (b, 0, 0, 0)),
                  pl.BlockSpec(new_blk, lambda b, p, pt: (b, 0, 0, 0)),
                  pl.BlockSpec(page_blk, lambda b, p, pt: (pt[b, jnp.minimum(p, n_pages - 1)], layer, 0, 0, 0)),
                  pl.BlockSpec(page_blk, lambda b, p, pt: (pt[b, jnp.maximum(p - n_pages, 0)], layer, 0, 0, 0))],
        out_specs=pl.BlockSpec(tok_blk, lambda b, p, pt: (b, 0, 0, 0)),
        scratch_shapes=[pltpu.VMEM((ATT_HEADS, LANES, ATT_HEAD_DIM), F32),
                        pltpu.VMEM((n_pages * PAGE_SIZE + PAGE_SIZE, LANES), F32),
                        pltpu.VMEM((n_pages * PAGE_SIZE // MOBA_BLOCK, ATT_HEADS, ATT_HEAD_DIM), F32),
                        pltpu.VMEM((1, LANES), F32),
                        pltpu.VMEM((ATT_HEADS, t_len, ATT_HEAD_DIM), F32)])
    return pl.pallas_call(
        functools.partial(_moba_sample_kernel, n_pages=n_pages, t_len=t_len),
        grid_spec=grid_spec,
        out_shape=jax.ShapeDtypeStruct((n_seq, ATT_HEADS, t_len, ATT_HEAD_DIM), F32),
        compiler_params=_cparams(("parallel", "arbitrary")),
        name="moba_sample",
    )(page_table, q_htd, k_thd, v_thd, cache_k, cache_v)


def _merge_kernel(x_ref, ya_ref, yb_ref, yc_ref, g0_ref, g1_ref, g2_ref, bg_ref, wa_ref, wb_ref, wc_ref, wo_ref, o_ref):
    bg = bg_ref[...]
    merged = (jax.nn.sigmoid(g0_ref[...] + bg[:, :D_MODEL]) * _dot(ya_ref[...], wa_ref[...])
              + jax.nn.sigmoid(g1_ref[...] + bg[:, D_MODEL:2 * D_MODEL]) * _dot(yb_ref[...], wb_ref[...])
              + jax.nn.sigmoid(g2_ref[...] + bg[:, 2 * D_MODEL:]) * _dot(yc_ref[...], wc_ref[...]))
    o_ref[...] = x_ref[...] + _dot(merged.astype(BF16), wo_ref[...])


def _merge(x2d, ya, yb, yc, proj, b_gate, wa, wb, wc, wo, tm):
    m = x2d.shape[0]
    const = lambda i: (0, 0)
    gate_spec = lambda k: pl.BlockSpec((tm, D_MODEL), lambda i: (i, C_GATE // D_MODEL + k))
    return pl.pallas_call(
        _merge_kernel,
        grid=(m // tm,),
        in_specs=[pl.BlockSpec((tm, D_MODEL), lambda i: (i, 0)),
                  pl.BlockSpec((tm, D_A), lambda i: (i, 0)),
                  pl.BlockSpec((tm, D_B), lambda i: (i, 0)),
                  pl.BlockSpec((tm, D_C), lambda i: (i, 0)),
                  gate_spec(0), gate_spec(1), gate_spec(2),
                  pl.BlockSpec((1, N_BRANCH * D_MODEL), const),
                  pl.BlockSpec((D_A, D_MODEL), const),
                  pl.BlockSpec((D_B, D_MODEL), const),
                  pl.BlockSpec((D_C, D_MODEL), const),
                  pl.BlockSpec((D_MODEL, D_MODEL), const)],
        out_specs=pl.BlockSpec((tm, D_MODEL), lambda i: (i, 0)),
        out_shape=jax.ShapeDtypeStruct((m, D_MODEL), F32),
        compiler_params=_cparams(("parallel",)),
        name="merge",
    )(x2d, ya, yb, yc, proj, proj, proj, b_gate, wa, wb, wc, wo)


def _ffn_kernel(x_ref, g_ref, wu_ref, wd_ref, gf_ref, o_ref, h_ref, acc_ref, *, final_norm):
    j = pl.program_id(1)

    @pl.when(j == 0)
    def _():
        h_ref[...] = _rms(x_ref[...], g_ref[...]).astype(BF16)
        acc_ref[...] = x_ref[...]

    f = jnp.maximum(_dot(h_ref[...], wu_ref[...]), 0.0)
    acc_ref[...] += _dot((f * f).astype(BF16), wd_ref[...])

    @pl.when(j == pl.num_programs(1) - 1)
    def _():
        y = acc_ref[...]
        o_ref[...] = _rms(y, gf_ref[...]) if final_norm else y


def _ffn(x2d, g, wu, wd, gf, tm, tf, final_norm):
    m = x2d.shape[0]
    return pl.pallas_call(
        functools.partial(_ffn_kernel, final_norm=final_norm),
        grid=(m // tm, D_FF // tf),
        in_specs=[pl.BlockSpec((tm, D_MODEL), lambda i, j: (i, 0)),
                  pl.BlockSpec((1, D_MODEL), lambda i, j: (0, 0)),
                  pl.BlockSpec((D_MODEL, tf), lambda i, j: (0, j)),
                  pl.BlockSpec((tf, D_MODEL), lambda i, j: (j, 0)),
                  pl.BlockSpec((1, D_MODEL), lambda i, j: (0, 0))],
        out_specs=pl.BlockSpec((tm, D_MODEL), lambda i, j: (i, 0)),
        out_shape=jax.ShapeDtypeStruct((m, D_MODEL), F32),
        scratch_shapes=[pltpu.VMEM((tm, D_MODEL), BF16), pltpu.VMEM((tm, D_MODEL), F32)],
        compiler_params=_cparams(("parallel", "arbitrary")),
        name="ffn",
    )(x2d, g, wu, wd, gf)


def _prep_layer(l, norm1_g, w_in, b_gate, ln_v_g, ln_v_b, w_spatial, b_spatial, w_a_out, conv_w, conv_b,
                dt_bias, a_log, d_skip, ssm_norm_g, w_b_out, w_c_out, w_o, norm2_g, w_up, w_down, t_len):
    w = w_in[l]
    o_dt = 2 * D_A + D_B + CONV_DIM
    o_q = o_dt + SSM_HEADS
    o_gate = o_q + 3 * D_C
    w_perm = jnp.concatenate(
        [w[:, :o_dt], w[:, o_gate:], w[:, o_q:o_gate], w[:, o_dt:o_q],
         jnp.zeros((D_MODEL, DT_W - SSM_HEADS), w.dtype)], axis=1).astype(BF16)
    tril = jnp.tril(jnp.ones((A_CHUNK, A_CHUNK), bool))
    w_tril = jnp.where(tril[None], w_spatial[l], 0)
    bs_exp = jnp.repeat(b_spatial[l].T, A_GROUP_DIM, axis=1)
    w_exp = jnp.repeat(jnp.transpose(w_tril[:, :t_len, :t_len], (2, 1, 0)), A_GROUP_DIM, axis=2)
    pad_h = lambda v: jnp.concatenate([v, jnp.zeros((LANES - SSM_HEADS,), v.dtype)])[None, :]
    return dict(
        norm1_g=norm1_g[l][None, :], w_in=w_perm, b_gate=b_gate[l][None, :],
        ln_g=ln_v_g[l][None, :], ln_b=ln_v_b[l][None, :],
        w_tril=w_tril.astype(BF16), bs_exp=bs_exp, w_exp=w_exp, bs_exp_s=bs_exp[:t_len],
        conv_w=conv_w[l], conv_b=conv_b[l][None, :], dt_bias=pad_h(dt_bias[l]), a_log=pad_h(a_log[l]),
        d_skip=jnp.repeat(d_skip[l], SSM_HEAD_DIM)[None, :], ssm_norm_g=ssm_norm_g[l][None, :],
        w_a_out=w_a_out[l].astype(BF16), w_b_out=w_b_out[l].astype(BF16), w_c_out=w_c_out[l].astype(BF16),
        w_o=w_o[l].astype(BF16), norm2_g=norm2_g[l][None, :],
        w_up=w_up[l].astype(BF16), w_down=w_down[l].astype(BF16))


def _rope_tables(pos):
    inv = jnp.power(jnp.float32(ROPE_THETA), -jnp.arange(0, ATT_HEAD_DIM, 2, dtype=F32) / ATT_HEAD_DIM)
    ang = pos.astype(F32)[:, None] * inv[None, :]
    cos, sin = jnp.cos(ang), jnp.sin(ang)
    cos_h = jnp.concatenate([cos, cos], axis=1)
    sin_h = jnp.concatenate([-sin, sin], axis=1)
    return jnp.tile(cos_h, (1, ATT_HEADS)), jnp.tile(sin_h, (1, ATT_HEADS))


def kernel(x_prompt, x_sample, cache_k, cache_v, state_ssm, state_conv, page_table, norm1_g, w_in, b_gate, ln_v_g, ln_v_b, w_spatial, b_spatial, w_a_out, conv_w, conv_b, dt_bias, a_log, d_skip, ssm_norm_g, w_b_out, w_c_out, w_o, norm2_g, w_up, w_down, norm_f_g):
    bp, s_len, _ = x_prompt.shape
    db, t_len, _ = x_sample.shape
    depth = w_in.shape[0]
    past_len = page_table.shape[1] * PAGE_SIZE
    mp, ms = bp * s_len, db * t_len
    cos_p, sin_p = _rope_tables(jnp.arange(s_len))
    cos_s, sin_s = _rope_tables(past_len + jnp.arange(t_len))
    cos_s, sin_s = jnp.tile(cos_s, (db, 1)), jnp.tile(sin_s, (db, 1))
    conv_zero = jnp.zeros((bp, SSM_CONV - 1, CONV_DIM), F32)
    h_zero = jnp.zeros((bp, D_B, SSM_STATE), F32)
    gf = norm_f_g[None, :]
    tm_p = 1024

    xp = x_prompt.reshape(mp, D_MODEL)
    xs = x_sample.reshape(ms, D_MODEL)
    outs = {k: [] for k in ("a_v_s", "ssm_p", "ssm_s", "conv_p", "conv_s", "k_p", "v_p", "k_s", "v_s")}
    for l in range(depth):
        w = _prep_layer(l, norm1_g, w_in, b_gate, ln_v_g, ln_v_b, w_spatial, b_spatial, w_a_out, conv_w, conv_b,
                        dt_bias, a_log, d_skip, ssm_norm_g, w_b_out, w_c_out, w_o, norm2_g, w_up, w_down, t_len)
        last = l == depth - 1

        proj = _in_proj(xp, w["norm1_g"], w["w_in"], tm_p)
        ya = _branch_a_prompt(proj, w["ln_g"], w["ln_b"], w["w_tril"], w["bs_exp"])
        yb, h_new = _branch_b(proj, bp, SSM_CHUNK, conv_zero, h_zero, w["conv_w"], w["conv_b"], w["dt_bias"],
                              w["a_log"], w["d_skip"], w["ssm_norm_g"])
        q_rot, k_rot, kmean = _rope_prompt(proj, bp, cos_p, sin_p)
        yc = _moba_prompt(q_rot, k_rot, proj, kmean, bp)
        x1 = _merge(xp, ya, yb, yc, proj, w["b_gate"], w["w_a_out"], w["w_b_out"], w["w_c_out"], w["w_o"], 512)
        xp = _ffn(x1, w["norm2_g"], w["w_up"], w["w_down"], gf, tm_p, 1024, last)
        proj3 = proj.reshape(bp, s_len, N_COLS)
        outs["ssm_p"].append(h_new.reshape(bp, SSM_HEADS, SSM_HEAD_DIM, SSM_STATE))
        outs["conv_p"].append(proj3[:, s_len - (SSM_CONV - 1):, C_XBC:C_XBC + CONV_DIM])
        outs["k_p"].append(k_rot.reshape(bp, s_len, ATT_HEADS, ATT_HEAD_DIM))
        outs["v_p"].append(proj3[:, :, C_VATT:C_VATT + D_C].reshape(bp, s_len, ATT_HEADS, ATT_HEAD_DIM))

        proj = _in_proj(xs, w["norm1_g"], w["w_in"], ms)
        ya, vn = _branch_a_sample(proj, t_len, w["ln_g"], w["ln_b"], w["w_exp"], w["bs_exp_s"])
        yb, h_new = _branch_b(proj, db, t_len, state_conv[l], state_ssm[l].reshape(db, D_B, SSM_STATE),
                              w["conv_w"], w["conv_b"], w["dt_bias"], w["a_log"], w["d_skip"], w["ssm_norm_g"])
        q_rot, k_rot = _rope_sample(proj, cos_s, sin_s)
        proj3 = proj.reshape(db, t_len, N_COLS)
        v_new = proj3[:, :, C_VATT:C_VATT + D_C].reshape(db, t_len, ATT_HEADS, ATT_HEAD_DIM)
        k_new = k_rot.reshape(db, t_len, ATT_HEADS, ATT_HEAD_DIM)
        q_htd = jnp.transpose(q_rot.reshape(db, t_len, ATT_HEADS, ATT_HEAD_DIM), (0, 2, 1, 3))
        o_htd = _moba_sample(q_htd, k_new, v_new, cache_k, cache_v, page_table, l)
        yc = jnp.transpose(o_htd, (0, 2, 1, 3)).reshape(ms, D_C).astype(BF16)
        x1 = _merge(xs, ya, yb, yc, proj, w["b_gate"], w["w_a_out"], w["w_b_out"], w["w_c_out"], w["w_o"], ms)
        xs = _ffn(x1, w["norm2_g"], w["w_up"], w["w_down"], gf, ms, 1024, last)
        outs["a_v_s"].append(vn.reshape(db, t_len, D_A))
        outs["ssm_s"].append(h_new.reshape(db, SSM_HEADS, SSM_HEAD_DIM, SSM_STATE))
        if t_len >= SSM_CONV - 1:
            conv_new = proj3[:, t_len - (SSM_CONV - 1):, C_XBC:C_XBC + CONV_DIM]
        else:
            conv_new = jnp.concatenate([state_conv[l], proj3[:, :, C_XBC:C_XBC + CONV_DIM]],
                                       axis=1)[:, -(SSM_CONV - 1):]
        outs["conv_s"].append(conv_new)
        outs["k_s"].append(k_new)
        outs["v_s"].append(v_new)

    st = lambda k: jnp.stack(outs[k])
    return (xp.reshape(bp, s_len, D_MODEL), xs.reshape(db, t_len, D_MODEL), st("a_v_s"), st("ssm_p"), st("ssm_s"),
            st("conv_p"), st("conv_s"), st("k_p"), st("v_p"), st("k_s"), st("v_s"))
```

```python
import functools
import math

import jax
import jax.numpy as jnp
from jax import lax
from jax.experimental import pallas as pl
from jax.experimental.pallas import tpu as pltpu

F32 = jnp.float32
BF16 = jnp.bfloat16
NEG_INF = float("-inf")

D_MODEL = 1024
PAGE_SIZE = 128
A_CHUNK = 128
D_A = D_MODEL // 2
A_GROUPS = 8
A_GROUP_DIM = D_A // A_GROUPS
D_B = D_MODEL
SSM_HEAD_DIM = 64
SSM_HEADS = D_B // SSM_HEAD_DIM
SSM_GROUPS = 4
SSM_STATE = 128
SSM_CONV = 4
SSM_CHUNK = 128
CONV_DIM = D_B + 2 * SSM_GROUPS * SSM_STATE
ATT_HEADS = 8
ATT_HEAD_DIM = 64
D_C = ATT_HEADS * ATT_HEAD_DIM
MOBA_BLOCK = 256
MOBA_TOPK = 3
ROPE_THETA = 10000.0
N_BRANCH = 3
D_FF = 4 * D_MODEL
EPS = 1e-6

LANES = 128
C_U, C_V, C_Z, C_XBC, C_GATE, C_Q, C_K, C_VATT, C_DT = 0, 512, 1024, 2048, 4096, 7168, 7680, 8192, 8704
DT_W = 256
N_COLS = C_DT + DT_W
IN_TN = 1280
VMEM_LIMIT = 56 * 1024 * 1024


def _cparams(sem):
    return pltpu.CompilerParams(dimension_semantics=sem, vmem_limit_bytes=VMEM_LIMIT)


def _dot(a, b):
    return jnp.dot(a, b, preferred_element_type=F32)


def _dot_nt(a, b):
    return lax.dot_general(a, b, (((1,), (1,)), ((), ())), preferred_element_type=F32)


def _split3(x):
    x1 = x.astype(BF16)
    r = x - x1.astype(F32)
    x2 = r.astype(BF16)
    r = r - x2.astype(F32)
    return x1, x2, r.astype(BF16)


def _dot_hi_nt(a, b):
    a1, a2, a3 = _split3(a)
    b1, b2, b3 = _split3(b)
    return (_dot_nt(a1, b1) + (_dot_nt(a1, b2) + _dot_nt(a2, b1))
            + (_dot_nt(a1, b3) + _dot_nt(a2, b2) + _dot_nt(a3, b1)))


def _rms(x, g):
    return x * lax.rsqrt(jnp.mean(x * x, axis=-1, keepdims=True) + EPS) * g


def _gelu(x):
    return 0.5 * x * (1.0 + lax.erf(x * math.sqrt(0.5)))


def _silu(x):
    return x * jax.nn.sigmoid(x)


def _inproj_kernel(x_ref, g_ref, w_ref, o_ref, h_ref):
    @pl.when(pl.program_id(1) == 0)
    def _():
        h_ref[...] = _rms(x_ref[...], g_ref[...]).astype(BF16)

    o_ref[...] = _dot(h_ref[...], w_ref[...])


def _in_proj(x2d, g, w_bf, tm):
    m = x2d.shape[0]
    return pl.pallas_call(
        _inproj_kernel,
        grid=(m // tm, N_COLS // IN_TN),
        in_specs=[pl.BlockSpec((tm, D_MODEL), lambda i, j: (i, 0)),
                  pl.BlockSpec((1, D_MODEL), lambda i, j: (0, 0)),
                  pl.BlockSpec((D_MODEL, IN_TN), lambda i, j: (0, j))],
        out_specs=pl.BlockSpec((tm, IN_TN), lambda i, j: (i, j)),
        out_shape=jax.ShapeDtypeStruct((m, N_COLS), F32),
        scratch_shapes=[pltpu.VMEM((tm, D_MODEL), BF16)],
        compiler_params=_cparams(("parallel", "arbitrary")),
        name="in_proj",
    )(x2d, g, w_bf)


def _layernorm(x, g, b):
    xc = x - jnp.mean(x, axis=-1, keepdims=True)
    return xc * lax.rsqrt(jnp.mean(xc * xc, axis=-1, keepdims=True) + EPS) * g + b


def _branch_a_kernel(u_ref, v_ref, g_ref, b_ref, w_ref, bs_ref, ya_ref):
    gu = _gelu(u_ref[...])
    vn = _layernorm(_gelu(v_ref[...]), g_ref[...], b_ref[...]).astype(BF16)
    first = lax.broadcasted_iota(jnp.int32, (A_CHUNK, LANES), 1) < A_GROUP_DIM
    parts = []
    for p in range(A_GROUPS // 2):
        vp = vn[:, p * LANES:(p + 1) * LANES]
        parts.append(jnp.where(first, _dot(w_ref[2 * p], vp), _dot(w_ref[2 * p + 1], vp)))
    mixed = jnp.concatenate(parts, axis=1) + bs_ref[...]
    ya_ref[...] = (gu * mixed).astype(BF16)


def _branch_a_prompt(proj, ln_g, ln_b, w_tril_bf, bs_exp):
    m = proj.shape[0]
    return pl.pallas_call(
        _branch_a_kernel,
        grid=(m // A_CHUNK,),
        in_specs=[pl.BlockSpec((A_CHUNK, D_A), lambda i: (i, C_U // D_A)),
                  pl.BlockSpec((A_CHUNK, D_A), lambda i: (i, C_V // D_A)),
                  pl.BlockSpec((1, D_A), lambda i: (0, 0)),
                  pl.BlockSpec((1, D_A), lambda i: (0, 0)),
                  pl.BlockSpec((A_GROUPS, A_CHUNK, A_CHUNK), lambda i: (0, 0, 0)),
                  pl.BlockSpec((A_CHUNK, D_A), lambda i: (0, 0))],
        out_specs=pl.BlockSpec((A_CHUNK, D_A), lambda i: (i, 0)),
        out_shape=jax.ShapeDtypeStruct((m, D_A), BF16),
        compiler_params=_cparams(("parallel",)),
        name="branch_a_prompt",
    )(proj, proj, ln_g, ln_b, w_tril_bf, bs_exp)


def _branch_a_sample_kernel(u_ref, v_ref, g_ref, b_ref, wexp_ref, bs_ref, ya_ref, vn_ref):
    gu = _gelu(u_ref[...])
    vn = _layernorm(_gelu(v_ref[...]), g_ref[...], b_ref[...])
    vn_ref[...] = vn
    mixed = bs_ref[...]
    for s in range(vn.shape[0]):
        mixed = mixed + wexp_ref[s] * vn[s:s + 1, :]
    ya_ref[...] = (gu * mixed).astype(BF16)


def _branch_a_sample(proj, t_len, ln_g, ln_b, w_exp, bs_exp):
    m = proj.shape[0]
    return pl.pallas_call(
        _branch_a_sample_kernel,
        grid=(m // t_len,),
        in_specs=[pl.BlockSpec((t_len, D_A), lambda i: (i, C_U // D_A)),
                  pl.BlockSpec((t_len, D_A), lambda i: (i, C_V // D_A)),
                  pl.BlockSpec((1, D_A), lambda i: (0, 0)),
                  pl.BlockSpec((1, D_A), lambda i: (0, 0)),
                  pl.BlockSpec((t_len, t_len, D_A), lambda i: (0, 0, 0)),
                  pl.BlockSpec((t_len, D_A), lambda i: (0, 0))],
        out_specs=[pl.BlockSpec((t_len, D_A), lambda i: (i, 0)),
                   pl.BlockSpec((t_len, D_A), lambda i: (i, 0))],
        out_shape=[jax.ShapeDtypeStruct((m, D_A), BF16), jax.ShapeDtypeStruct((m, D_A), F32)],
        compiler_params=_cparams(("parallel",)),
        name="branch_a_sample",
    )(proj, proj, ln_g, ln_b, w_exp, bs_exp)


XP_OFF = 8


def _ssd_kernel(z_ref, xbc_ref, dt_ref, cprev_ref, h0_ref, cw_ref, cb_ref, dtb_ref, alog_ref, dsk_ref, ng_ref,
                yb_ref, hout_ref, xp_ref, *, rows):
    t_len = SSM_CHUNK
    c = pl.program_id(1)
    xbc, z, dtraw = xbc_ref[...], z_ref[...], dt_ref[...][:, :LANES]
    if rows < t_len:
        def pad(a):
            return jnp.concatenate([a, jnp.zeros((t_len - rows, a.shape[1]), a.dtype)], axis=0)
        xbc, z, dtraw = pad(xbc), pad(z), pad(dtraw)

    @pl.when(c == 0)
    def _():
        xp_ref[XP_OFF - 3:XP_OFF, :] = cprev_ref[...]
        hout_ref[...] = h0_ref[...]

    @pl.when(c > 0)
    def _():
        xp_ref[XP_OFF - 3:XP_OFF, :] = xp_ref[XP_OFF + t_len - 3:XP_OFF + t_len, :]

    xp_ref[XP_OFF:XP_OFF + t_len, :] = xbc
    acc = cb_ref[...] + xbc * cw_ref[SSM_CONV - 1:SSM_CONV, :]
    for tap in range(SSM_CONV - 1):
        lo = XP_OFF - (SSM_CONV - 1) + tap
        acc = acc + xp_ref[lo:lo + t_len, :] * cw_ref[tap:tap + 1, :]
    act = _silu(acc)
    xs = act[:, :D_B]
    bm = act[:, D_B:D_B + SSM_GROUPS * SSM_STATE].astype(BF16)
    cm = act[:, D_B + SSM_GROUPS * SSM_STATE:].astype(BF16)

    dt = jax.nn.softplus(dtraw + dtb_ref[...])
    row_i = lax.broadcasted_iota(jnp.int32, (t_len, t_len), 0)
    col_i = lax.broadcasted_iota(jnp.int32, (t_len, t_len), 1)
    if rows < t_len:
        dt = jnp.where(lax.broadcasted_iota(jnp.int32, dt.shape, 0) < rows, dt, 0.0)
    a = dt * (-jnp.exp(alog_ref[...]))
    tril = jnp.where(col_i <= row_i, 1.0, 0.0).astype(BF16)
    a1, a2, a3 = _split3(a)
    acs = _dot(tril, a1) + _dot(tril, a2) + _dot(tril, a3)
    acs_t = acs.T
    dt_t = dt.T
    xs_t = xs.T
    causal_t = row_i <= col_i
    rep = SSM_HEADS // SSM_GROUPS
    ys = []
    for g in range(SSM_GROUPS):
        bg = bm[:, g * SSM_STATE:(g + 1) * SSM_STATE]
        cg = cm[:, g * SSM_STATE:(g + 1) * SSM_STATE]
        cb_t = _dot_nt(bg, cg)
        for r in range(rep):
            h = g * rep + r
            hs = slice(h * SSM_HEAD_DIM, (h + 1) * SSM_HEAD_DIM)
            acs_row = acs_t[h:h + 1, :]
            acs_col = acs[:, h:h + 1]
            dec_t = jnp.exp(jnp.where(causal_t, acs_row - acs_col, NEG_INF))
            w_t = (cb_t * dec_t).astype(BF16)
            xdt_t = xs_t[hs, :] * dt_t[h:h + 1, :]
            h_in = hout_ref[hs, :]
            y_t = _dot(xdt_t.astype(BF16), w_t) + _dot_nt(h_in.astype(BF16), cg) * jnp.exp(acs_row)
            last = acs_row[:, t_len - 1:t_len]
            st = _dot((xdt_t * jnp.exp(last - acs_row)).astype(BF16), bg)
            hout_ref[hs, :] = jnp.exp(last) * h_in + st
            ys.append(y_t)
    y = jnp.concatenate(ys, axis=0).T + dsk_ref[...] * xs
    yb = _rms(y * _silu(z), ng_ref[...])
    yb_ref[...] = yb[:rows].astype(BF16)


def _branch_b(proj, n_seq, rows, conv_prev, h0, conv_w, conv_b, dt_bias_p, a_log_p, d_skip_exp, norm_g):
    m = proj.shape[0]
    n_chunk = m // (n_seq * rows)
    const = lambda b, c: (0, 0)
    return pl.pallas_call(
        functools.partial(_ssd_kernel, rows=rows),
        grid=(n_seq, n_chunk),
        in_specs=[pl.BlockSpec((rows, D_B), lambda b, c: (b * n_chunk + c, C_Z // D_B)),
                  pl.BlockSpec((rows, CONV_DIM), lambda b, c: (b * n_chunk + c, C_XBC // CONV_DIM)),
                  pl.BlockSpec((rows, DT_W), lambda b, c: (b * n_chunk + c, C_DT // DT_W)),
                  pl.BlockSpec((None, SSM_CONV - 1, CONV_DIM), lambda b, c: (b, 0, 0)),
                  pl.BlockSpec((None, D_B, SSM_STATE), lambda b, c: (b, 0, 0)),
                  pl.BlockSpec((SSM_CONV, CONV_DIM), const),
                  pl.BlockSpec((1, CONV_DIM), const),
                  pl.BlockSpec((1, LANES), const),
                  pl.BlockSpec((1, LANES), const),
                  pl.BlockSpec((1, D_B), const),
                  pl.BlockSpec((1, D_B), const)],
        out_specs=[pl.BlockSpec((rows, D_B), lambda b, c: (b * n_chunk + c, 0)),
                   pl.BlockSpec((None, D_B, SSM_STATE), lambda b, c: (b, 0, 0))],
        out_shape=[jax.ShapeDtypeStruct((m, D_B), BF16),
                   jax.ShapeDtypeStruct((n_seq, D_B, SSM_STATE), F32)],
        scratch_shapes=[pltpu.VMEM((XP_OFF + SSM_CHUNK, CONV_DIM), F32)],
        compiler_params=_cparams(("parallel", "arbitrary")),
        name="branch_b_rows%d" % rows,
    )(proj, proj, proj, conv_prev, h0, conv_w, conv_b, dt_bias_p, a_log_p, d_skip_exp, norm_g)


def _rope_apply(x, cos, sin_signed, first):
    partner = jnp.where(first, pltpu.roll(x, D_C - ATT_HEAD_DIM // 2, 1), pltpu.roll(x, ATT_HEAD_DIM // 2, 1))
    return x * cos + partner * sin_signed


def _rope_kernel(q_ref, k_ref, cos_ref, sin_ref, qo_ref, ko_ref):
    lane = lax.broadcasted_iota(jnp.int32, q_ref.shape, 1)
    first = (lane % ATT_HEAD_DIM) < ATT_HEAD_DIM // 2
    cos, sin = cos_ref[...], sin_ref[...]
    qo_ref[...] = _rope_apply(q_ref[...], cos, sin, first)
    ko_ref[...] = _rope_apply(k_ref[...], cos, sin, first)


def _rope_gate_kernel(q_ref, k_ref, cos_ref, sin_ref, qo_ref, ko_ref, bias_ref, km_ref, *, nblk):
    j = pl.program_id(1)
    lane = lax.broadcasted_iota(jnp.int32, q_ref.shape, 1)
    first = (lane % ATT_HEAD_DIM) < ATT_HEAD_DIM // 2
    cos, sin = cos_ref[...], sin_ref[...]
    qr = _rope_apply(q_ref[...], cos, sin, first)
    kr = _rope_apply(k_ref[...], cos, sin, first)
    qo_ref[...] = qr
    ko_ref[...] = kr

    @pl.when(j == 0)
    def _():
        km_ref[...] = jnp.zeros(km_ref.shape, F32)

    km = km_ref[...]
    km_ref[pl.ds(j, 1), :] = jnp.mean(kr, axis=0, keepdims=True)
    klane = lax.broadcasted_iota(jnp.int32, km.shape, 1) // ATT_HEAD_DIM
    kstack = jnp.concatenate([jnp.where(klane == h, km, 0.0) for h in range(ATT_HEADS)], axis=0)
    gate = _dot_hi_nt(kstack, qr)
    bidx = lax.broadcasted_iota(jnp.int32, (nblk, MOBA_BLOCK), 0)
    past = bidx < j
    n_sel = max(1, min(MOBA_TOPK, nblk - 1))
    for h in range(ATT_HEADS):
        g = jnp.where(past, gate[h * nblk:(h + 1) * nblk, :], NEG_INF)
        sel = past & (_topk_rank(g, bidx, nblk - 1, 0) < n_sel)
        bias_ref[h * nblk:(h + 1) * nblk, :] = jnp.where(sel, 0.0, NEG_INF)


def _rope_prompt(proj, n_seq, cos, sin):
    m = proj.shape[0]
    nblk = m // (n_seq * MOBA_BLOCK)
    row = lambda b, j: (b * nblk + j, 0)
    return pl.pallas_call(
        functools.partial(_rope_gate_kernel, nblk=nblk),
        grid=(n_seq, nblk),
        in_specs=[pl.BlockSpec((MOBA_BLOCK, D_C), lambda b, j: (b * nblk + j, C_Q // D_C)),
                  pl.BlockSpec((MOBA_BLOCK, D_C), lambda b, j: (b * nblk + j, C_K // D_C)),
                  pl.BlockSpec((MOBA_BLOCK, D_C), lambda b, j: (j, 0)),
                  pl.BlockSpec((MOBA_BLOCK, D_C), lambda b, j: (j, 0))],
        out_specs=[pl.BlockSpec((MOBA_BLOCK, D_C), row),
                   pl.BlockSpec((MOBA_BLOCK, D_C), row),
                   pl.BlockSpec((None, ATT_HEADS * nblk, MOBA_BLOCK), lambda b, j: (b, 0, j))],
        out_shape=[jax.ShapeDtypeStruct((m, D_C), F32), jax.ShapeDtypeStruct((m, D_C), F32),
                   jax.ShapeDtypeStruct((n_seq, ATT_HEADS * nblk, m // n_seq), F32)],
        scratch_shapes=[pltpu.VMEM((nblk, D_C), F32)],
        compiler_params=_cparams(("parallel", "arbitrary")),
        name="rope_prompt",
    )(proj, proj, cos, sin)


def _rope_sample(proj, cos, sin):
    m = proj.shape[0]
    return pl.pallas_call(
        _rope_kernel,
        grid=(1,),
        in_specs=[pl.BlockSpec((m, D_C), lambda i: (0, C_Q // D_C)),
                  pl.BlockSpec((m, D_C), lambda i: (0, C_K // D_C)),
                  pl.BlockSpec((m, D_C), lambda i: (0, 0)),
                  pl.BlockSpec((m, D_C), lambda i: (0, 0))],
        out_specs=[pl.BlockSpec((m, D_C), lambda i: (0, 0)), pl.BlockSpec((m, D_C), lambda i: (0, 0))],
        out_shape=[jax.ShapeDtypeStruct((m, D_C), F32), jax.ShapeDtypeStruct((m, D_C), F32)],
        compiler_params=_cparams(("arbitrary",)),
        name="rope_sample",
    )(proj, proj, cos, sin)


def _topk_rank(gate, idx, n_cand, axis):
    rank = jnp.zeros(gate.shape, jnp.int32)
    for i in range(n_cand):
        gi = gate[:, i:i + 1] if axis == 1 else gate[i:i + 1, :]
        beats = (gi > gate) | ((gi == gate) & (i < idx))
        rank = rank + jnp.where(beats, 1, 0)
    return rank


SUBLANES = 8


def _fold_rows(x, op):
    return op(x.reshape(x.shape[0] // SUBLANES, SUBLANES, x.shape[1]), axis=0)


def _moba_prompt_kernel(q_ref, k_ref, v_ref, bias_ref, o_ref, kb_ref, vt_ref, *, nblk):
    qb = pl.program_id(2)
    n_head = LANES // ATT_HEAD_DIM

    @pl.when(qb == 0)
    def _():
        for j in range(nblk):
            kb_ref[j] = k_ref[j * MOBA_BLOCK:(j + 1) * MOBA_BLOCK, :].astype(BF16)
            vt_ref[j] = v_ref[j * MOBA_BLOCK:(j + 1) * MOBA_BLOCK, :].T.astype(BF16)

    q = q_ref[...] * (ATT_HEAD_DIM ** -0.5)
    lane = lax.broadcasted_iota(jnp.int32, (MOBA_BLOCK, LANES), 1)
    own_bias = jnp.where(lax.broadcasted_iota(jnp.int32, (MOBA_BLOCK, MOBA_BLOCK), 0)
                         <= lax.broadcasted_iota(jnp.int32, (MOBA_BLOCK, MOBA_BLOCK), 1), 0.0, NEG_INF)
    row = lax.broadcasted_iota(jnp.int32, (LANES, MOBA_BLOCK), 0)
    qms = [jnp.where((lane >= h * ATT_HEAD_DIM) & (lane < (h + 1) * ATT_HEAD_DIM), q, 0.0).astype(BF16)
           for h in range(n_head)]

    def attend(n_past):
        outs = []
        for h in range(n_head):
            sts = [_dot_nt(kb_ref[j], qms[h]) + bias_ref[h * nblk + j:h * nblk + j + 1, :] for j in range(n_past)]
            sts.append(_dot_nt(kb_ref[n_past], qms[h]) + own_bias)
            mx = _fold_rows(sts[0], jnp.max)
            for st in sts[1:]:
                mx = jnp.maximum(mx, _fold_rows(st, jnp.max))
            m = jnp.max(mx, axis=0, keepdims=True)
            lsum = jnp.zeros((SUBLANES, MOBA_BLOCK), F32)
            ot = jnp.zeros((LANES, MOBA_BLOCK), F32)
            for j, st in enumerate(sts):
                p = jnp.exp(st - m)
                lsum = lsum + _fold_rows(p, jnp.sum)
                ot = ot + _dot(vt_ref[j], p.astype(BF16))
            outs.append(ot / jnp.sum(lsum, axis=0, keepdims=True))
        o_ref[...] = jnp.where(row < ATT_HEAD_DIM, outs[0], outs[1]).T.astype(BF16)

    for n_past in range(nblk):
        pl.when(qb == n_past)(functools.partial(attend, n_past))


def _moba_prompt(q_rot, k_rot, proj, bias, n_seq):
    m = q_rot.shape[0]
    s_len = m // n_seq
    nblk = s_len // MOBA_BLOCK
    n_pair = D_C // LANES
    n_head = LANES // ATT_HEAD_DIM
    return pl.pallas_call(
        functools.partial(_moba_prompt_kernel, nblk=nblk),
        grid=(n_seq, n_pair, nblk),
        in_specs=[pl.BlockSpec((MOBA_BLOCK, LANES), lambda b, hp, i: (b * nblk + i, hp)),
                  pl.BlockSpec((s_len, LANES), lambda b, hp, i: (b, hp)),
                  pl.BlockSpec((s_len, LANES), lambda b, hp, i: (b, C_VATT // LANES + hp)),
                  pl.BlockSpec((None, n_head * nblk, MOBA_BLOCK), lambda b, hp, i: (b, hp, i))],
        out_specs=pl.BlockSpec((MOBA_BLOCK, LANES), lambda b, hp, i: (b * nblk + i, hp)),
        out_shape=jax.ShapeDtypeStruct((m, D_C), BF16),
        scratch_shapes=[pltpu.VMEM((nblk, MOBA_BLOCK, LANES), BF16),
                        pltpu.VMEM((nblk, LANES, MOBA_BLOCK), BF16)],
        compiler_params=_cparams(("parallel", "parallel", "arbitrary")),
        name="moba_prompt",
    )(q_rot, k_rot, proj, bias)


SAMPLE_PAGE_GROUP = 8


def _moba_sample_kernel(pt_ref, q_ref, kn_ref, vn_ref, *rest, n_pages, t_len):
    del pt_ref
    grp = SAMPLE_PAGE_GROUP
    ck_refs, cv_refs = rest[:grp], rest[grp:2 * grp]
    o_ref, s_ref, own_ref, oacc_ref = rest[2 * grp:]
    step = pl.program_id(1)
    n_steps = n_pages // grp
    scale = ATT_HEAD_DIM ** -0.5
    ppb = MOBA_BLOCK // PAGE_SIZE
    n_blk = n_pages // ppb
    n_row = ATT_HEADS * t_len
    rows = lambda h: slice(h * t_len, (h + 1) * t_len)

    @pl.when(step < n_steps)
    def _():
        for i in range(grp):
            for h in range(ATT_HEADS):
                sc = _dot(q_ref[h].astype(BF16), ck_refs[i][h].astype(BF16))
                s_ref[step * grp + i, rows(h), :] = sc * scale

    @pl.when(step == n_steps - 1)
    def _():
        lane = lax.broadcasted_iota(jnp.int32, (n_row, LANES), 1)
        gate = jnp.full((n_row, LANES), NEG_INF, F32)
        for j in range(n_blk):
            tot = s_ref[ppb * j]
            for i in range(1, ppb):
                tot = tot + s_ref[ppb * j + i]
            gate = jnp.where(lane == j, jnp.sum(tot, axis=1, keepdims=True), gate)
        sel = jnp.where(_topk_rank(gate, lane, n_blk, 1) < min(MOBA_TOPK, n_blk), 1, 0)
        zpad = jnp.zeros((PAGE_SIZE - t_len, ATT_HEAD_DIM), F32)
        for h in range(ATT_HEADS):
            kh = jnp.concatenate([kn_ref[:, h, :], zpad], axis=0).astype(BF16)
            own_ref[rows(h), :] = _dot_nt(q_ref[h].astype(BF16), kh) * scale
        tok = lax.broadcasted_iota(jnp.int32, (n_row, LANES), 0) % t_len
        so = jnp.where(lane <= tok, own_ref[...], NEG_INF)
        mv = so
        for j in range(n_blk):
            for i in range(ppb):
                mv = jnp.maximum(mv, jnp.where(sel[:, j:j + 1] > 0, s_ref[ppb * j + i], NEG_INF))
        m = jnp.max(mv, axis=1, keepdims=True)
        po = jnp.exp(so - m)
        own_ref[...] = po
        lv = po
        for j in range(n_blk):
            for i in range(ppb):
                pj = jnp.exp(jnp.where(sel[:, j:j + 1] > 0, s_ref[ppb * j + i], NEG_INF) - m)
                s_ref[ppb * j + i] = pj
                lv = lv + pj
        inv = 1.0 / jnp.sum(lv, axis=1, keepdims=True)
        for h in range(ATT_HEADS):
            vh = jnp.concatenate([vn_ref[:, h, :], zpad], axis=0).astype(BF16)
            oacc_ref[h] = _dot(own_ref[rows(h), :].astype(BF16), vh)
        own_ref[:, 0:1] = inv

    @pl.when(step >= n_steps)
    def _():
        for i in range(grp):
            page = (step - n_steps) * grp + i
            for h in range(ATT_HEADS):
                pm = s_ref[page, rows(h), :].astype(BF16)
                oacc_ref[h] = oacc_ref[h] + _dot_nt(pm, cv_refs[i][h].astype(BF16))

    @pl.when(step == 2 * n_steps - 1)
    def _():
        for h in range(ATT_HEADS):
            o_ref[h] = oacc_ref[h] * own_ref[rows(h), 0:1]


def _moba_sample(q_htd, k_thd, v_thd, cache_kt, cache_vt, page_table, layer):
    n_seq, _, t_len, _ = q_htd.shape
    n_pages = page_table.shape[1]
    grp = SAMPLE_PAGE_GROUP
    n_steps = n_pages // grp
    page_blk = (None, None, ATT_HEADS, ATT_HEAD_DIM, PAGE_SIZE)
    tok_blk = (None, ATT_HEADS, t_len, ATT_HEAD_DIM)
    new_blk = (None, t_len, ATT_HEADS, ATT_HEAD_DIM)

    def k_spec(i):
        return pl.BlockSpec(page_blk, lambda b, s, pt: (pt[b, jnp.minimum(s, n_steps - 1) * grp + i], layer, 0, 0, 0))

    def v_spec(i):
        return pl.BlockSpec(page_blk, lambda b, s, pt: (pt[b, jnp.maximum(s - n_steps, 0) * grp + i], layer, 0, 0, 0))

    grid_spec = pltpu.PrefetchScalarGridSpec(
        num_scalar_prefetch=1,
        grid=(n_seq, 2 * n_steps),
        in_specs=[pl.BlockSpec(tok_blk, lambda b, s, pt: (b, 0, 0, 0)),
                  pl.BlockSpec(new_blk, lambda b, s, pt: (b, 0, 0, 0)),
                  pl.BlockSpec(new_blk, lambda b, s, pt: (b, 0, 0, 0))]
        + [k_spec(i) for i in range(grp)] + [v_spec(i) for i in range(grp)],
        out_specs=pl.BlockSpec(tok_blk, lambda b, s, pt: (b, 0, 0, 0)),
        scratch_shapes=[pltpu.VMEM((n_pages, ATT_HEADS * t_len, PAGE_SIZE), F32),
                        pltpu.VMEM((ATT_HEADS * t_len, PAGE_SIZE), F32),
                        pltpu.VMEM((ATT_HEADS, t_len, ATT_HEAD_DIM), F32)])
    return pl.pallas_call(
        functools.partial(_moba_sample_kernel, n_pages=n_pages, t_len=t_len),
        grid_spec=grid_spec,
        out_shape=jax.ShapeDtypeStruct((n_seq, ATT_HEADS, t_len, ATT_HEAD_DIM), F32),
        compiler_params=_cparams(("parallel", "arbitrary")),
        name="moba_sample",
    )(page_table, q_htd, k_thd, v_thd, *([cache_kt] * grp), *([cache_vt] * grp))


def _merge_kernel(x_ref, ya_ref, yb_ref, yc_ref, g0_ref, g1_ref, g2_ref, bg_ref, wa_ref, wb_ref, wc_ref, wo_ref, o_ref):
    bg = bg_ref[...]
    merged = (jax.nn.sigmoid(g0_ref[...] + bg[:, :D_MODEL]) * _dot(ya_ref[...], wa_ref[...])
              + jax.nn.sigmoid(g1_ref[...] + bg[:, D_MODEL:2 * D_MODEL]) * _dot(yb_ref[...], wb_ref[...])
              + jax.nn.sigmoid(g2_ref[...] + bg[:, 2 * D_MODEL:]) * _dot(yc_ref[...], wc_ref[...]))
    o_ref[...] = x_ref[...] + _dot(merged.astype(BF16), wo_ref[...])


def _merge(x2d, ya, yb, yc, proj, b_gate, wa, wb, wc, wo, tm):
    m = x2d.shape[0]
    const = lambda i: (0, 0)
    gate_spec = lambda k: pl.BlockSpec((tm, D_MODEL), lambda i: (i, C_GATE // D_MODEL + k))
    return pl.pallas_call(
        _merge_kernel,
        grid=(m // tm,),
        in_specs=[pl.BlockSpec((tm, D_MODEL), lambda i: (i, 0)),
                  pl.BlockSpec((tm, D_A), lambda i: (i, 0)),
                  pl.BlockSpec((tm, D_B), lambda i: (i, 0)),
                  pl.BlockSpec((tm, D_C), lambda i: (i, 0)),
                  gate_spec(0), gate_spec(1), gate_spec(2),
                  pl.BlockSpec((1, N_BRANCH * D_MODEL), const),
                  pl.BlockSpec((D_A, D_MODEL), const),
                  pl.BlockSpec((D_B, D_MODEL), const),
                  pl.BlockSpec((D_C, D_MODEL), const),
                  pl.BlockSpec((D_MODEL, D_MODEL), const)],
        out_specs=pl.BlockSpec((tm, D_MODEL), lambda i: (i, 0)),
        out_shape=jax.ShapeDtypeStruct((m, D_MODEL), F32),
        compiler_params=_cparams(("parallel",)),
        name="merge",
    )(x2d, ya, yb, yc, proj, proj, proj, b_gate, wa, wb, wc, wo)


def _ffn_kernel(x_ref, g_ref, wu_ref, wd_ref, gf_ref, o_ref, h_ref, acc_ref, *, final_norm):
    j = pl.program_id(1)

    @pl.when(j == 0)
    def _():
        h_ref[...] = _rms(x_ref[...], g_ref[...]).astype(BF16)
        acc_ref[...] = x_ref[...]

    f = jnp.maximum(_dot(h_ref[...], wu_ref[...]), 0.0)
    acc_ref[...] += _dot((f * f).astype(BF16), wd_ref[...])

    @pl.when(j == pl.num_programs(1) - 1)
    def _():
        y = acc_ref[...]
        o_ref[...] = _rms(y, gf_ref[...]) if final_norm else y


def _ffn(x2d, g, wu, wd, gf, tm, tf, final_norm):
    m = x2d.shape[0]
    return pl.pallas_call(
        functools.partial(_ffn_kernel, final_norm=final_norm),
        grid=(m // tm, D_FF // tf),
        in_specs=[pl.BlockSpec((tm, D_MODEL), lambda i, j: (i, 0)),
                  pl.BlockSpec((1, D_MODEL), lambda i, j: (0, 0)),
                  pl.BlockSpec((D_MODEL, tf), lambda i, j: (0, j)),
                  pl.BlockSpec((tf, D_MODEL), lambda i, j: (j, 0)),
                  pl.BlockSpec((1, D_MODEL), lambda i, j: (0, 0))],
        out_specs=pl.BlockSpec((tm, D_MODEL), lambda i, j: (i, 0)),
        out_shape=jax.ShapeDtypeStruct((m, D_MODEL), F32),
        scratch_shapes=[pltpu.VMEM((tm, D_MODEL), BF16), pltpu.VMEM((tm, D_MODEL), F32)],
        compiler_params=_cparams(("parallel", "arbitrary")),
        name="ffn",
    )(x2d, g, wu, wd, gf)


def _prep_layer(l, norm1_g, w_in, b_gate, ln_v_g, ln_v_b, w_spatial, b_spatial, w_a_out, conv_w, conv_b,
                dt_bias, a_log, d_skip, ssm_norm_g, w_b_out, w_c_out, w_o, norm2_g, w_up, w_down, t_len):
    w = w_in[l]
    o_dt = 2 * D_A + D_B + CONV_DIM
    o_q = o_dt + SSM_HEADS
    o_gate = o_q + 3 * D_C
    w_perm = jnp.concatenate(
        [w[:, :o_dt], w[:, o_gate:], w[:, o_q:o_gate], w[:, o_dt:o_q],
         jnp.zeros((D_MODEL, DT_W - SSM_HEADS), w.dtype)], axis=1).astype(BF16)
    tril = jnp.tril(jnp.ones((A_CHUNK, A_CHUNK), bool))
    w_tril = jnp.where(tril[None], w_spatial[l], 0)
    bs_exp = jnp.repeat(b_spatial[l].T, A_GROUP_DIM, axis=1)
    w_exp = jnp.repeat(jnp.transpose(w_tril[:, :t_len, :t_len], (2, 1, 0)), A_GROUP_DIM, axis=2)
    pad_h = lambda v: jnp.concatenate([v, jnp.zeros((LANES - SSM_HEADS,), v.dtype)])[None, :]
    return dict(
        norm1_g=norm1_g[l][None, :], w_in=w_perm, b_gate=b_gate[l][None, :],
        ln_g=ln_v_g[l][None, :], ln_b=ln_v_b[l][None, :],
        w_tril=w_tril.astype(BF16), bs_exp=bs_exp, w_exp=w_exp, bs_exp_s=bs_exp[:t_len],
        conv_w=conv_w[l], conv_b=conv_b[l][None, :], dt_bias=pad_h(dt_bias[l]), a_log=pad_h(a_log[l]),
        d_skip=jnp.repeat(d_skip[l], SSM_HEAD_DIM)[None, :], ssm_norm_g=ssm_norm_g[l][None, :],
        w_a_out=w_a_out[l].astype(BF16), w_b_out=w_b_out[l].astype(BF16), w_c_out=w_c_out[l].astype(BF16),
        w_o=w_o[l].astype(BF16), norm2_g=norm2_g[l][None, :],
        w_up=w_up[l].astype(BF16), w_down=w_down[l].astype(BF16))


def _rope_tables(pos):
    inv = jnp.power(jnp.float32(ROPE_THETA), -jnp.arange(0, ATT_HEAD_DIM, 2, dtype=F32) / ATT_HEAD_DIM)
    ang = pos.astype(F32)[:, None] * inv[None, :]
    cos, sin = jnp.cos(ang), jnp.sin(ang)
    cos_h = jnp.concatenate([cos, cos], axis=1)
    sin_h = jnp.concatenate([-sin, sin], axis=1)
    return jnp.tile(cos_h, (1, ATT_HEADS)), jnp.tile(sin_h, (1, ATT_HEADS))


def kernel(x_prompt, x_sample, cache_k, cache_v, state_ssm, state_conv, page_table, norm1_g, w_in, b_gate, ln_v_g, ln_v_b, w_spatial, b_spatial, w_a_out, conv_w, conv_b, dt_bias, a_log, d_skip, ssm_norm_g, w_b_out, w_c_out, w_o, norm2_g, w_up, w_down, norm_f_g):
    bp, s_len, _ = x_prompt.shape
    db, t_len, _ = x_sample.shape
    depth = w_in.shape[0]
    past_len = page_table.shape[1] * PAGE_SIZE
    mp, ms = bp * s_len, db * t_len
    cos_p, sin_p = _rope_tables(jnp.arange(s_len))
    cos_s, sin_s = _rope_tables(past_len + jnp.arange(t_len))
    cos_s, sin_s = jnp.tile(cos_s, (db, 1)), jnp.tile(sin_s, (db, 1))
    conv_zero = jnp.zeros((bp, SSM_CONV - 1, CONV_DIM), F32)
    h_zero = jnp.zeros((bp, D_B, SSM_STATE), F32)
    gf = norm_f_g[None, :]
    tm_p = 1024
    cache_kt = jnp.transpose(cache_k, (0, 1, 3, 4, 2))
    cache_vt = jnp.transpose(cache_v, (0, 1, 3, 4, 2))

    xp = x_prompt.reshape(mp, D_MODEL)
    xs = x_sample.reshape(ms, D_MODEL)
    outs = {k: [] for k in ("a_v_s", "ssm_p", "ssm_s", "conv_p", "conv_s", "k_p", "v_p", "k_s", "v_s")}
    for l in range(depth):
        w = _prep_layer(l, norm1_g, w_in, b_gate, ln_v_g, ln_v_b, w_spatial, b_spatial, w_a_out, conv_w, conv_b,
                        dt_bias, a_log, d_skip, ssm_norm_g, w_b_out, w_c_out, w_o, norm2_g, w_up, w_down, t_len)
        last = l == depth - 1

        proj = _in_proj(xp, w["norm1_g"], w["w_in"], tm_p)
        ya = _branch_a_prompt(proj, w["ln_g"], w["ln_b"], w["w_tril"], w["bs_exp"])
        yb, h_new = _branch_b(proj, bp, SSM_CHUNK, conv_zero, h_zero, w["conv_w"], w["conv_b"], w["dt_bias"],
                              w["a_log"], w["d_skip"], w["ssm_norm_g"])
        q_rot, k_rot, sel_bias = _rope_prompt(proj, bp, cos_p, sin_p)
        yc = _moba_prompt(q_rot, k_rot, proj, sel_bias, bp)
        x1 = _merge(xp, ya, yb, yc, proj, w["b_gate"], w["w_a_out"], w["w_b_out"], w["w_c_out"], w["w_o"], 512)
        xp = _ffn(x1, w["norm2_g"], w["w_up"], w["w_down"], gf, tm_p, 1024, last)
        proj3 = proj.reshape(bp, s_len, N_COLS)
        outs["ssm_p"].append(h_new.reshape(bp, SSM_HEADS, SSM_HEAD_DIM, SSM_STATE))
        outs["conv_p"].append(proj3[:, s_len - (SSM_CONV - 1):, C_XBC:C_XBC + CONV_DIM])
        outs["k_p"].append(k_rot.reshape(bp, s_len, ATT_HEADS, ATT_HEAD_DIM))
        outs["v_p"].append(proj3[:, :, C_VATT:C_VATT + D_C].reshape(bp, s_len, ATT_HEADS, ATT_HEAD_DIM))

        proj = _in_proj(xs, w["norm1_g"], w["w_in"], ms)
        ya, vn = _branch_a_sample(proj, t_len, w["ln_g"], w["ln_b"], w["w_exp"], w["bs_exp_s"])
        yb, h_new = _branch_b(proj, db, t_len, state_conv[l], state_ssm[l].reshape(db, D_B, SSM_STATE),
                              w["conv_w"], w["conv_b"], w["dt_bias"], w["a_log"], w["d_skip"], w["ssm_norm_g"])
        q_rot, k_rot = _rope_sample(proj, cos_s, sin_s)
        proj3 = proj.reshape(db, t_len, N_COLS)
        v_new = proj3[:, :, C_VATT:C_VATT + D_C].reshape(db, t_len, ATT_HEADS, ATT_HEAD_DIM)
        k_new = k_rot.reshape(db, t_len, ATT_HEADS, ATT_HEAD_DIM)
        q_htd = jnp.transpose(q_rot.reshape(db, t_len, ATT_HEADS, ATT_HEAD_DIM), (0, 2, 1, 3))
        o_htd = _moba_sample(q_htd, k_new, v_new, cache_kt, cache_vt, page_table, l)
        yc = jnp.transpose(o_htd, (0, 2, 1, 3)).reshape(ms, D_C).astype(BF16)
        x1 = _merge(xs, ya, yb, yc, proj, w["b_gate"], w["w_a_out"], w["w_b_out"], w["w_c_out"], w["w_o"], ms)
        xs = _ffn(x1, w["norm2_g"], w["w_up"], w["w_down"], gf, ms, 1024, last)
        outs["a_v_s"].append(vn.reshape(db, t_len, D_A))
        outs["ssm_s"].append(h_new.reshape(db, SSM_HEADS, SSM_HEAD_DIM, SSM_STATE))
        if t_len >= SSM_CONV - 1:
            conv_new = proj3[:, t_len - (SSM_CONV - 1):, C_XBC:C_XBC + CONV_DIM]
        else:
            conv_new = jnp.concatenate([state_conv[l], proj3[:, :, C_XBC:C_XBC + CONV_DIM]],
                                       axis=1)[:, -(SSM_CONV - 1):]
        outs["conv_s"].append(conv_new)
        outs["k_s"].append(k_new)
        outs["v_s"].append(v_new)

    st = lambda k: jnp.stack(outs[k])
    return (xp.reshape(bp, s_len, D_MODEL), xs.reshape(db, t_len, D_MODEL), st("a_v_s"), st("ssm_p"), st("ssm_s"),
            st("conv_p"), st("conv_s"), st("k_p"), st("v_p"), st("k_s"), st("v_s"))
```

```python
import functools
import math

import jax
import jax.numpy as jnp
from jax import lax
from jax.experimental import pallas as pl
from jax.experimental.pallas import tpu as pltpu

F32 = jnp.float32
BF16 = jnp.bfloat16
NEG_INF = float("-inf")

D_MODEL = 1024
PAGE_SIZE = 128
A_CHUNK = 128
D_A = D_MODEL // 2
A_GROUPS = 8
A_GROUP_DIM = D_A // A_GROUPS
D_B = D_MODEL
SSM_HEAD_DIM = 64
SSM_HEADS = D_B // SSM_HEAD_DIM
SSM_GROUPS = 4
SSM_STATE = 128
SSM_CONV = 4
SSM_CHUNK = 128
CONV_DIM = D_B + 2 * SSM_GROUPS * SSM_STATE
ATT_HEADS = 8
ATT_HEAD_DIM = 64
D_C = ATT_HEADS * ATT_HEAD_DIM
MOBA_BLOCK = 256
MOBA_TOPK = 3
ROPE_THETA = 10000.0
N_BRANCH = 3
D_FF = 4 * D_MODEL
EPS = 1e-6

LANES = 128
C_U, C_V, C_Z, C_XBC, C_GATE, C_Q, C_K, C_VATT, C_DT = 0, 512, 1024, 2048, 4096, 7168, 7680, 8192, 8704
DT_W = 256
N_COLS = C_DT + DT_W
IN_TN = 1280
VMEM_LIMIT = 56 * 1024 * 1024


def _cparams(sem):
    return pltpu.CompilerParams(dimension_semantics=sem, vmem_limit_bytes=VMEM_LIMIT)


def _dot(a, b):
    return jnp.dot(a, b, preferred_element_type=F32)


def _dot_nt(a, b):
    return lax.dot_general(a, b, (((1,), (1,)), ((), ())), preferred_element_type=F32)


def _split3(x):
    x1 = x.astype(BF16)
    r = x - x1.astype(F32)
    x2 = r.astype(BF16)
    r = r - x2.astype(F32)
    return x1, x2, r.astype(BF16)


def _dot_hi_nt(a, b):
    a1, a2, a3 = _split3(a)
    b1, b2, b3 = _split3(b)
    return (_dot_nt(a1, b1) + (_dot_nt(a1, b2) + _dot_nt(a2, b1))
            + (_dot_nt(a1, b3) + _dot_nt(a2, b2) + _dot_nt(a3, b1)))


def _rms(x, g):
    return x * lax.rsqrt(jnp.mean(x * x, axis=-1, keepdims=True) + EPS) * g


def _gelu(x):
    return 0.5 * x * (1.0 + lax.erf(x * math.sqrt(0.5)))


def _silu(x):
    return x * jax.nn.sigmoid(x)


def _inproj_kernel(x_ref, g_ref, w_ref, o_ref, v_ref, h_ref):
    j = pl.program_id(1)

    @pl.when(j == 0)
    def _():
        h_ref[...] = _rms(x_ref[...], g_ref[...]).astype(BF16)

    o_ref[...] = _dot(h_ref[...], w_ref[...])

    @pl.when(j == C_VATT // IN_TN)
    def _():
        v_ref[...] = o_ref[:, C_VATT % IN_TN:C_VATT % IN_TN + D_C]


def _in_proj(x2d, g, w_bf, tm):
    m = x2d.shape[0]
    assert C_VATT // IN_TN == (C_VATT + D_C - 1) // IN_TN
    return pl.pallas_call(
        _inproj_kernel,
        grid=(m // tm, N_COLS // IN_TN),
        in_specs=[pl.BlockSpec((tm, D_MODEL), lambda i, j: (i, 0)),
                  pl.BlockSpec((1, D_MODEL), lambda i, j: (0, 0)),
                  pl.BlockSpec((D_MODEL, IN_TN), lambda i, j: (0, j))],
        out_specs=[pl.BlockSpec((tm, IN_TN), lambda i, j: (i, j)),
                   pl.BlockSpec((tm, D_C), lambda i, j: (i, 0))],
        out_shape=[jax.ShapeDtypeStruct((m, N_COLS), F32), jax.ShapeDtypeStruct((m, D_C), F32)],
        scratch_shapes=[pltpu.VMEM((tm, D_MODEL), BF16)],
        compiler_params=_cparams(("parallel", "arbitrary")),
        name="in_proj",
    )(x2d, g, w_bf)


def _layernorm(x, g, b):
    xc = x - jnp.mean(x, axis=-1, keepdims=True)
    return xc * lax.rsqrt(jnp.mean(xc * xc, axis=-1, keepdims=True) + EPS) * g + b


def _branch_a_kernel(u_ref, v_ref, g_ref, b_ref, w_ref, bs_ref, ya_ref):
    gu = _gelu(u_ref[...])
    vn = _layernorm(_gelu(v_ref[...]), g_ref[...], b_ref[...]).astype(BF16)
    first = lax.broadcasted_iota(jnp.int32, (A_CHUNK, LANES), 1) < A_GROUP_DIM
    parts = []
    for p in range(A_GROUPS // 2):
        vp = vn[:, p * LANES:(p + 1) * LANES]
        parts.append(jnp.where(first, _dot(w_ref[2 * p], vp), _dot(w_ref[2 * p + 1], vp)))
    mixed = jnp.concatenate(parts, axis=1) + bs_ref[...]
    ya_ref[...] = (gu * mixed).astype(BF16)


def _branch_a_prompt(proj, ln_g, ln_b, w_tril_bf, bs_exp):
    m = proj.shape[0]
    return pl.pallas_call(
        _branch_a_kernel,
        grid=(m // A_CHUNK,),
        in_specs=[pl.BlockSpec((A_CHUNK, D_A), lambda i: (i, C_U // D_A)),
                  pl.BlockSpec((A_CHUNK, D_A), lambda i: (i, C_V // D_A)),
                  pl.BlockSpec((1, D_A), lambda i: (0, 0)),
                  pl.BlockSpec((1, D_A), lambda i: (0, 0)),
                  pl.BlockSpec((A_GROUPS, A_CHUNK, A_CHUNK), lambda i: (0, 0, 0)),
                  pl.BlockSpec((A_CHUNK, D_A), lambda i: (0, 0))],
        out_specs=pl.BlockSpec((A_CHUNK, D_A), lambda i: (i, 0)),
        out_shape=jax.ShapeDtypeStruct((m, D_A), BF16),
        compiler_params=_cparams(("parallel",)),
        name="branch_a_prompt",
    )(proj, proj, ln_g, ln_b, w_tril_bf, bs_exp)


def _branch_a_sample_kernel(u_ref, v_ref, g_ref, b_ref, wexp_ref, bs_ref, ya_ref, vn_ref):
    gu = _gelu(u_ref[...])
    vn = _layernorm(_gelu(v_ref[...]), g_ref[...], b_ref[...])
    vn_ref[...] = vn
    mixed = bs_ref[...]
    for s in range(vn.shape[0]):
        mixed = mixed + wexp_ref[s] * vn[s:s + 1, :]
    ya_ref[...] = (gu * mixed).astype(BF16)


def _branch_a_sample(proj, t_len, ln_g, ln_b, w_exp, bs_exp):
    m = proj.shape[0]
    return pl.pallas_call(
        _branch_a_sample_kernel,
        grid=(m // t_len,),
        in_specs=[pl.BlockSpec((t_len, D_A), lambda i: (i, C_U // D_A)),
                  pl.BlockSpec((t_len, D_A), lambda i: (i, C_V // D_A)),
                  pl.BlockSpec((1, D_A), lambda i: (0, 0)),
                  pl.BlockSpec((1, D_A), lambda i: (0, 0)),
                  pl.BlockSpec((t_len, t_len, D_A), lambda i: (0, 0, 0)),
                  pl.BlockSpec((t_len, D_A), lambda i: (0, 0))],
        out_specs=[pl.BlockSpec((t_len, D_A), lambda i: (i, 0)),
                   pl.BlockSpec((t_len, D_A), lambda i: (i, 0))],
        out_shape=[jax.ShapeDtypeStruct((m, D_A), BF16), jax.ShapeDtypeStruct((m, D_A), F32)],
        compiler_params=_cparams(("parallel",)),
        name="branch_a_sample",
    )(proj, proj, ln_g, ln_b, w_exp, bs_exp)


XP_OFF = 8


def _ssd_kernel(z_ref, xbc_ref, dt_ref, cprev_ref, h0_ref, cw_ref, cb_ref, dtb_ref, alog_ref, dsk_ref, ng_ref,
                yb_ref, hout_ref, xp_ref, *, rows):
    t_len = SSM_CHUNK
    c = pl.program_id(1)
    xbc, z, dtraw = xbc_ref[...], z_ref[...], dt_ref[...][:, :LANES]
    if rows < t_len:
        def pad(a):
            return jnp.concatenate([a, jnp.zeros((t_len - rows, a.shape[1]), a.dtype)], axis=0)
        xbc, z, dtraw = pad(xbc), pad(z), pad(dtraw)

    @pl.when(c == 0)
    def _():
        xp_ref[XP_OFF - 3:XP_OFF, :] = cprev_ref[...]
        hout_ref[...] = h0_ref[...]

    @pl.when(c > 0)
    def _():
        xp_ref[XP_OFF - 3:XP_OFF, :] = xp_ref[XP_OFF + t_len - 3:XP_OFF + t_len, :]

    xp_ref[XP_OFF:XP_OFF + t_len, :] = xbc
    acc = cb_ref[...] + xbc * cw_ref[SSM_CONV - 1:SSM_CONV, :]
    for tap in range(SSM_CONV - 1):
        lo = XP_OFF - (SSM_CONV - 1) + tap
        acc = acc + xp_ref[lo:lo + t_len, :] * cw_ref[tap:tap + 1, :]
    act = _silu(acc)
    xs = act[:, :D_B]
    bm = act[:, D_B:D_B + SSM_GROUPS * SSM_STATE].astype(BF16)
    cm = act[:, D_B + SSM_GROUPS * SSM_STATE:].astype(BF16)

    dt = jax.nn.softplus(dtraw + dtb_ref[...])
    row_i = lax.broadcasted_iota(jnp.int32, (t_len, t_len), 0)
    col_i = lax.broadcasted_iota(jnp.int32, (t_len, t_len), 1)
    if rows < t_len:
        dt = jnp.where(lax.broadcasted_iota(jnp.int32, dt.shape, 0) < rows, dt, 0.0)
    a = dt * (-jnp.exp(alog_ref[...]))
    tril = jnp.where(col_i <= row_i, 1.0, 0.0).astype(BF16)
    a1, a2, a3 = _split3(a)
    acs = _dot(tril, a1) + _dot(tril, a2) + _dot(tril, a3)
    acs_t = acs.T
    dt_t = dt.T
    xs_t = xs.T
    causal_t = row_i <= col_i
    rep = SSM_HEADS // SSM_GROUPS
    ys = []
    for g in range(SSM_GROUPS):
        bg = bm[:, g * SSM_STATE:(g + 1) * SSM_STATE]
        cg = cm[:, g * SSM_STATE:(g + 1) * SSM_STATE]
        cb_t = _dot_nt(bg, cg)
        for r in range(rep):
            h = g * rep + r
            hs = slice(h * SSM_HEAD_DIM, (h + 1) * SSM_HEAD_DIM)
            acs_row = acs_t[h:h + 1, :]
            acs_col = acs[:, h:h + 1]
            dec_t = jnp.exp(jnp.where(causal_t, acs_row - acs_col, NEG_INF))
            w_t = (cb_t * dec_t).astype(BF16)
            xdt_t = xs_t[hs, :] * dt_t[h:h + 1, :]
            h_in = hout_ref[hs, :]
            y_t = _dot(xdt_t.astype(BF16), w_t) + _dot_nt(h_in.astype(BF16), cg) * jnp.exp(acs_row)
            last = acs_row[:, t_len - 1:t_len]
            st = _dot((xdt_t * jnp.exp(last - acs_row)).astype(BF16), bg)
            hout_ref[hs, :] = jnp.exp(last) * h_in + st
            ys.append(y_t)
    y = jnp.concatenate(ys, axis=0).T + dsk_ref[...] * xs
    yb = _rms(y * _silu(z), ng_ref[...])
    yb_ref[...] = yb[:rows].astype(BF16)


def _branch_b(proj, n_seq, rows, conv_prev, h0, layer, conv_w, conv_b, dt_bias_p, a_log_p, d_skip_exp, norm_g):
    m = proj.shape[0]
    n_chunk = m // (n_seq * rows)
    const = lambda b, c: (0, 0)
    return pl.pallas_call(
        functools.partial(_ssd_kernel, rows=rows),
        grid=(n_seq, n_chunk),
        in_specs=[pl.BlockSpec((rows, D_B), lambda b, c: (b * n_chunk + c, C_Z // D_B)),
                  pl.BlockSpec((rows, CONV_DIM), lambda b, c: (b * n_chunk + c, C_XBC // CONV_DIM)),
                  pl.BlockSpec((rows, DT_W), lambda b, c: (b * n_chunk + c, C_DT // DT_W)),
                  pl.BlockSpec((None, None, SSM_CONV - 1, CONV_DIM), lambda b, c: (layer, b, 0, 0)),
                  pl.BlockSpec((None, None, D_B, SSM_STATE), lambda b, c: (layer, b, 0, 0)),
                  pl.BlockSpec((SSM_CONV, CONV_DIM), const),
                  pl.BlockSpec((1, CONV_DIM), const),
                  pl.BlockSpec((1, LANES), const),
                  pl.BlockSpec((1, LANES), const),
                  pl.BlockSpec((1, D_B), const),
                  pl.BlockSpec((1, D_B), const)],
        out_specs=[pl.BlockSpec((rows, D_B), lambda b, c: (b * n_chunk + c, 0)),
                   pl.BlockSpec((None, D_B, SSM_STATE), lambda b, c: (b, 0, 0))],
        out_shape=[jax.ShapeDtypeStruct((m, D_B), BF16),
                   jax.ShapeDtypeStruct((n_seq, D_B, SSM_STATE), F32)],
        scratch_shapes=[pltpu.VMEM((XP_OFF + SSM_CHUNK, CONV_DIM), F32)],
        compiler_params=_cparams(("parallel", "arbitrary")),
        name="branch_b_rows%d" % rows,
    )(proj, proj, proj, conv_prev, h0, conv_w, conv_b, dt_bias_p, a_log_p, d_skip_exp, norm_g)


def _rope_apply(x, cos, sin_signed, first):
    partner = jnp.where(first, pltpu.roll(x, D_C - ATT_HEAD_DIM // 2, 1), pltpu.roll(x, ATT_HEAD_DIM // 2, 1))
    return x * cos + partner * sin_signed


def _rope_kernel(q_ref, k_ref, cos_ref, sin_ref, qo_ref, ko_ref):
    lane = lax.broadcasted_iota(jnp.int32, q_ref.shape, 1)
    first = (lane % ATT_HEAD_DIM) < ATT_HEAD_DIM // 2
    cos, sin = cos_ref[...], sin_ref[...]
    qo_ref[...] = _rope_apply(q_ref[...], cos, sin, first)
    ko_ref[...] = _rope_apply(k_ref[...], cos, sin, first)


def _rope_gate_kernel(q_ref, k_ref, cos_ref, sin_ref, qo_ref, ko_ref, bias_ref, km_ref, *, nblk):
    j = pl.program_id(1)
    lane = lax.broadcasted_iota(jnp.int32, q_ref.shape, 1)
    first = (lane % ATT_HEAD_DIM) < ATT_HEAD_DIM // 2
    cos, sin = cos_ref[...], sin_ref[...]
    qr = _rope_apply(q_ref[...], cos, sin, first)
    kr = _rope_apply(k_ref[...], cos, sin, first)
    qo_ref[...] = qr
    ko_ref[...] = kr

    @pl.when(j == 0)
    def _():
        km_ref[...] = jnp.zeros(km_ref.shape, F32)

    km = km_ref[...]
    km_ref[pl.ds(j, 1), :] = jnp.mean(kr, axis=0, keepdims=True)
    klane = lax.broadcasted_iota(jnp.int32, km.shape, 1) // ATT_HEAD_DIM
    kstack = jnp.concatenate([jnp.where(klane == h, km, 0.0) for h in range(ATT_HEADS)], axis=0)
    gate = _dot_hi_nt(kstack, qr)
    bidx = lax.broadcasted_iota(jnp.int32, (nblk, MOBA_BLOCK), 0)
    past = bidx < j
    n_sel = max(1, min(MOBA_TOPK, nblk - 1))
    for h in range(ATT_HEADS):
        g = jnp.where(past, gate[h * nblk:(h + 1) * nblk, :], NEG_INF)
        sel = past & (_topk_rank(g, bidx, nblk - 1, 0) < n_sel)
        bias_ref[h * nblk:(h + 1) * nblk, :] = jnp.where(sel, 0.0, NEG_INF)


def _rope_prompt(proj, n_seq, cos, sin):
    m = proj.shape[0]
    nblk = m // (n_seq * MOBA_BLOCK)
    row = lambda b, j: (b * nblk + j, 0)
    return pl.pallas_call(
        functools.partial(_rope_gate_kernel, nblk=nblk),
        grid=(n_seq, nblk),
        in_specs=[pl.BlockSpec((MOBA_BLOCK, D_C), lambda b, j: (b * nblk + j, C_Q // D_C)),
                  pl.BlockSpec((MOBA_BLOCK, D_C), lambda b, j: (b * nblk + j, C_K // D_C)),
                  pl.BlockSpec((MOBA_BLOCK, D_C), lambda b, j: (j, 0)),
                  pl.BlockSpec((MOBA_BLOCK, D_C), lambda b, j: (j, 0))],
        out_specs=[pl.BlockSpec((MOBA_BLOCK, D_C), row),
                   pl.BlockSpec((MOBA_BLOCK, D_C), row),
                   pl.BlockSpec((None, ATT_HEADS * nblk, MOBA_BLOCK), lambda b, j: (b, 0, j))],
        out_shape=[jax.ShapeDtypeStruct((m, D_C), F32), jax.ShapeDtypeStruct((m, D_C), F32),
                   jax.ShapeDtypeStruct((n_seq, ATT_HEADS * nblk, m // n_seq), F32)],
        scratch_shapes=[pltpu.VMEM((nblk, D_C), F32)],
        compiler_params=_cparams(("parallel", "arbitrary")),
        name="rope_prompt",
    )(proj, proj, cos, sin)


def _rope_sample(proj, cos, sin):
    m = proj.shape[0]
    return pl.pallas_call(
        _rope_kernel,
        grid=(1,),
        in_specs=[pl.BlockSpec((m, D_C), lambda i: (0, C_Q // D_C)),
                  pl.BlockSpec((m, D_C), lambda i: (0, C_K // D_C)),
                  pl.BlockSpec((m, D_C), lambda i: (0, 0)),
                  pl.BlockSpec((m, D_C), lambda i: (0, 0))],
        out_specs=[pl.BlockSpec((m, D_C), lambda i: (0, 0)), pl.BlockSpec((m, D_C), lambda i: (0, 0))],
        out_shape=[jax.ShapeDtypeStruct((m, D_C), F32), jax.ShapeDtypeStruct((m, D_C), F32)],
        compiler_params=_cparams(("arbitrary",)),
        name="rope_sample",
    )(proj, proj, cos, sin)


def _topk_rank(gate, idx, n_cand, axis):
    rank = jnp.zeros(gate.shape, jnp.int32)
    for i in range(n_cand):
        gi = gate[:, i:i + 1] if axis == 1 else gate[i:i + 1, :]
        beats = (gi > gate) | ((gi == gate) & (i < idx))
        rank = rank + jnp.where(beats, 1, 0)
    return rank


SUBLANES = 8


def _fold_rows(x, op):
    return op(x.reshape(x.shape[0] // SUBLANES, SUBLANES, x.shape[1]), axis=0)


def _moba_prompt_kernel(q_ref, k_ref, v_ref, bias_ref, o_ref, kb_ref, vt_ref, *, nblk):
    qb = pl.program_id(2)
    n_head = LANES // ATT_HEAD_DIM

    @pl.when(qb == 0)
    def _():
        for j in range(nblk):
            kb_ref[j] = k_ref[j * MOBA_BLOCK:(j + 1) * MOBA_BLOCK, :].astype(BF16)
            vt_ref[j] = v_ref[j * MOBA_BLOCK:(j + 1) * MOBA_BLOCK, :].T.astype(BF16)

    q = q_ref[...] * (ATT_HEAD_DIM ** -0.5)
    lane = lax.broadcasted_iota(jnp.int32, (MOBA_BLOCK, LANES), 1)
    own_bias = jnp.where(lax.broadcasted_iota(jnp.int32, (MOBA_BLOCK, MOBA_BLOCK), 0)
                         <= lax.broadcasted_iota(jnp.int32, (MOBA_BLOCK, MOBA_BLOCK), 1), 0.0, NEG_INF)
    row = lax.broadcasted_iota(jnp.int32, (LANES, MOBA_BLOCK), 0)
    qms = [jnp.where((lane >= h * ATT_HEAD_DIM) & (lane < (h + 1) * ATT_HEAD_DIM), q, 0.0).astype(BF16)
           for h in range(n_head)]

    def attend(n_past):
        outs = []
        for h in range(n_head):
            sts = [_dot_nt(kb_ref[j], qms[h]) + bias_ref[h * nblk + j:h * nblk + j + 1, :] for j in range(n_past)]
            sts.append(_dot_nt(kb_ref[n_past], qms[h]) + own_bias)
            mx = _fold_rows(sts[0], jnp.max)
            for st in sts[1:]:
                mx = jnp.maximum(mx, _fold_rows(st, jnp.max))
            m = jnp.max(mx, axis=0, keepdims=True)
            lsum = jnp.zeros((SUBLANES, MOBA_BLOCK), F32)
            ot = jnp.zeros((LANES, MOBA_BLOCK), F32)
            for j, st in enumerate(sts):
                p = jnp.exp(st - m)
                lsum = lsum + _fold_rows(p, jnp.sum)
                ot = ot + _dot(vt_ref[j], p.astype(BF16))
            outs.append(ot / jnp.sum(lsum, axis=0, keepdims=True))
        o_ref[...] = jnp.where(row < ATT_HEAD_DIM, outs[0], outs[1]).T.astype(BF16)

    for n_past in range(nblk):
        pl.when(qb == n_past)(functools.partial(attend, n_past))


def _moba_prompt(q_rot, k_rot, v_att, bias, n_seq):
    m = q_rot.shape[0]
    s_len = m // n_seq
    nblk = s_len // MOBA_BLOCK
    n_pair = D_C // LANES
    n_head = LANES // ATT_HEAD_DIM
    return pl.pallas_call(
        functools.partial(_moba_prompt_kernel, nblk=nblk),
        grid=(n_seq, n_pair, nblk),
        in_specs=[pl.BlockSpec((MOBA_BLOCK, LANES), lambda b, hp, i: (b * nblk + i, hp)),
                  pl.BlockSpec((s_len, LANES), lambda b, hp, i: (b, hp)),
                  pl.BlockSpec((s_len, LANES), lambda b, hp, i: (b, hp)),
                  pl.BlockSpec((None, n_head * nblk, MOBA_BLOCK), lambda b, hp, i: (b, hp, i))],
        out_specs=pl.BlockSpec((MOBA_BLOCK, LANES), lambda b, hp, i: (b * nblk + i, hp)),
        out_shape=jax.ShapeDtypeStruct((m, D_C), BF16),
        scratch_shapes=[pltpu.VMEM((nblk, MOBA_BLOCK, LANES), BF16),
                        pltpu.VMEM((nblk, LANES, MOBA_BLOCK), BF16)],
        compiler_params=_cparams(("parallel", "parallel", "arbitrary")),
        name="moba_prompt",
    )(q_rot, k_rot, v_att, bias)


SAMPLE_PAGE_BUFFERS = 32
SAMPLE_PAGE_UNROLL = 8


def _moba_sample_kernel(pt_ref, q_ref, kn_ref, vn_ref, ck_hbm, cv_hbm, o_ref,
                        buf_ref, sem, s_ref, own_ref, oacc_ref, *, n_seq, n_pages, t_len, layer):
    nbuf = SAMPLE_PAGE_BUFFERS
    per_seq = 2 * n_pages
    total = n_seq * per_seq
    scale = ATT_HEAD_DIM ** -0.5
    ppb = MOBA_BLOCK // PAGE_SIZE
    n_blk = n_pages // ppb
    n_row = ATT_HEADS * t_len
    rows = lambda h: slice(h * t_len, (h + 1) * t_len)

    def page_copy(src_hbm, page, slot):
        return pltpu.make_async_copy(src_hbm.at[page, layer], buf_ref.at[slot], sem.at[slot])

    def start_fetch(n, slot):
        b = n // per_seq
        i = n % per_seq

        @pl.when(i < n_pages)
        def _():
            page_copy(ck_hbm, pt_ref[b, i], slot).start()

        @pl.when(i >= n_pages)
        def _():
            page_copy(cv_hbm, pt_ref[b, i - n_pages], slot).start()

    def finish_fetch(slot):
        page_copy(ck_hbm, 0, slot).wait()

    def refill(n, slot):
        @pl.when(n + nbuf < total)
        def _():
            start_fetch(n + nbuf, slot)

    b = pl.program_id(0)

    @pl.when(b == 0)
    def _():
        for n in range(nbuf):
            start_fetch(n, n)

    def one_sequence():
        base = b * per_seq

        grp = SAMPLE_PAGE_UNROLL

        def score_pages(g, c):
            pages = [g * grp + u for u in range(grp)]
            for i in pages:
                finish_fetch(i % nbuf)
            for i in pages:
                for h in range(ATT_HEADS):
                    sc = _dot(q_ref[h].astype(BF16), buf_ref[i % nbuf, h].astype(BF16))
                    s_ref[i, rows(h), :] = sc * scale
            for i in pages:
                refill(base + i, i % nbuf)
            return c

        lax.fori_loop(0, n_pages // grp, score_pages, 0)
        select_and_normalise(q_ref, kn_ref, vn_ref)

        def weigh_pages(g, accs):
            pages = [g * grp + u for u in range(grp)]
            for i in pages:
                finish_fetch((n_pages + i) % nbuf)
            accs = tuple(
                accs[h] + _dot_nt(
                    jnp.concatenate([s_ref[i, rows(h), :] for i in pages], axis=1).astype(BF16),
                    jnp.concatenate([buf_ref[(n_pages + i) % nbuf, h] for i in pages], axis=1).astype(BF16))
                for h in range(ATT_HEADS))
            for i in pages:
                refill(base + n_pages + i, (n_pages + i) % nbuf)
            return accs

        accs = lax.fori_loop(0, n_pages // grp, weigh_pages, tuple(oacc_ref[h] for h in range(ATT_HEADS)))
        for h in range(ATT_HEADS):
            o_ref[h] = accs[h] * own_ref[rows(h), 0:1]

    def select_and_normalise(q_ref, kn_ref, vn_ref):
        lane = lax.broadcasted_iota(jnp.int32, (n_row, LANES), 1)
        gate = jnp.full((n_row, LANES), NEG_INF, F32)
        for j in range(n_blk):
            tot = s_ref[ppb * j]
            for i in range(1, ppb):
                tot = tot + s_ref[ppb * j + i]
            gate = jnp.where(lane == j, jnp.sum(tot, axis=1, keepdims=True), gate)
        sel = jnp.where(_topk_rank(gate, lane, n_blk, 1) < min(MOBA_TOPK, n_blk), 1, 0)
        zpad = jnp.zeros((PAGE_SIZE - t_len, ATT_HEAD_DIM), F32)
        for h in range(ATT_HEADS):
            kh = jnp.concatenate([kn_ref[:, h, :], zpad], axis=0).astype(BF16)
            own_ref[rows(h), :] = _dot_nt(q_ref[h].astype(BF16), kh) * scale
        tok = lax.broadcasted_iota(jnp.int32, (n_row, LANES), 0) % t_len
        so = jnp.where(lane <= tok, own_ref[...], NEG_INF)
        mv = so
        for j in range(n_blk):
            for i in range(ppb):
                mv = jnp.maximum(mv, jnp.where(sel[:, j:j + 1] > 0, s_ref[ppb * j + i], NEG_INF))
        m = jnp.max(mv, axis=1, keepdims=True)
        po = jnp.exp(so - m)
        own_ref[...] = po
        lv = po
        for j in range(n_blk):
            for i in range(ppb):
                pj = jnp.exp(jnp.where(sel[:, j:j + 1] > 0, s_ref[ppb * j + i], NEG_INF) - m)
                s_ref[ppb * j + i] = pj
                lv = lv + pj
        inv = 1.0 / jnp.sum(lv, axis=1, keepdims=True)
        for h in range(ATT_HEADS):
            vh = jnp.concatenate([vn_ref[:, h, :], zpad], axis=0).astype(BF16)
            oacc_ref[h] = _dot(own_ref[rows(h), :].astype(BF16), vh)
        own_ref[:, 0:1] = inv

    one_sequence()


def _moba_sample(q_htd, k_thd, v_thd, cache_kt, cache_vt, page_table, layer):
    n_seq, _, t_len, _ = q_htd.shape
    n_pages = page_table.shape[1]
    assert (2 * n_pages) % SAMPLE_PAGE_BUFFERS == 0 and n_pages % SAMPLE_PAGE_UNROLL == 0
    tok_shape = (n_seq, ATT_HEADS, t_len, ATT_HEAD_DIM)
    per_seq = lambda shape: pl.BlockSpec((None,) + shape, lambda b, pt: (b, 0, 0, 0))
    tok_blk = per_seq((ATT_HEADS, t_len, ATT_HEAD_DIM))
    new_blk = per_seq((t_len, ATT_HEADS, ATT_HEAD_DIM))
    grid_spec = pltpu.PrefetchScalarGridSpec(
        num_scalar_prefetch=1,
        grid=(n_seq,),
        in_specs=[tok_blk, new_blk, new_blk, pl.BlockSpec(memory_space=pl.ANY), pl.BlockSpec(memory_space=pl.ANY)],
        out_specs=tok_blk,
        scratch_shapes=[pltpu.VMEM((SAMPLE_PAGE_BUFFERS, ATT_HEADS, ATT_HEAD_DIM, PAGE_SIZE), F32),
                        pltpu.SemaphoreType.DMA((SAMPLE_PAGE_BUFFERS,)),
                        pltpu.VMEM((n_pages, ATT_HEADS * t_len, PAGE_SIZE), F32),
                        pltpu.VMEM((ATT_HEADS * t_len, PAGE_SIZE), F32),
                        pltpu.VMEM((ATT_HEADS, t_len, ATT_HEAD_DIM), F32)])
    return pl.pallas_call(
        functools.partial(_moba_sample_kernel, n_seq=n_seq, n_pages=n_pages, t_len=t_len, layer=layer),
        grid_spec=grid_spec,
        out_shape=jax.ShapeDtypeStruct(tok_shape, F32),
        compiler_params=_cparams(("arbitrary",)),
        name="moba_sample",
    )(page_table, q_htd, k_thd, v_thd, cache_kt, cache_vt)


def _merge_kernel(x_ref, ya_ref, yb_ref, yc_ref, g0_ref, g1_ref, g2_ref, bg_ref, wa_ref, wb_ref, wc_ref, wo_ref, o_ref):
    bg = bg_ref[...]
    merged = (jax.nn.sigmoid(g0_ref[...] + bg[:, :D_MODEL]) * _dot(ya_ref[...], wa_ref[...])
              + jax.nn.sigmoid(g1_ref[...] + bg[:, D_MODEL:2 * D_MODEL]) * _dot(yb_ref[...], wb_ref[...])
              + jax.nn.sigmoid(g2_ref[...] + bg[:, 2 * D_MODEL:]) * _dot(yc_ref[...], wc_ref[...]))
    o_ref[...] = x_ref[...] + _dot(merged.astype(BF16), wo_ref[...])


def _merge(x2d, ya, yb, yc, proj, b_gate, wa, wb, wc, wo, tm):
    m = x2d.shape[0]
    const = lambda i: (0, 0)
    gate_spec = lambda k: pl.BlockSpec((tm, D_MODEL), lambda i: (i, C_GATE // D_MODEL + k))
    return pl.pallas_call(
        _merge_kernel,
        grid=(m // tm,),
        in_specs=[pl.BlockSpec((tm, D_MODEL), lambda i: (i, 0)),
                  pl.BlockSpec((tm, D_A), lambda i: (i, 0)),
                  pl.BlockSpec((tm, D_B), lambda i: (i, 0)),
                  pl.BlockSpec((tm, D_C), lambda i: (i, 0)),
                  gate_spec(0), gate_spec(1), gate_spec(2),
                  pl.BlockSpec((1, N_BRANCH * D_MODEL), const),
                  pl.BlockSpec((D_A, D_MODEL), const),
                  pl.BlockSpec((D_B, D_MODEL), const),
                  pl.BlockSpec((D_C, D_MODEL), const),
                  pl.BlockSpec((D_MODEL, D_MODEL), const)],
        out_specs=pl.BlockSpec((tm, D_MODEL), lambda i: (i, 0)),
        out_shape=jax.ShapeDtypeStruct((m, D_MODEL), F32),
        compiler_params=_cparams(("parallel",)),
        name="merge",
    )(x2d, ya, yb, yc, proj, proj, proj, b_gate, wa, wb, wc, wo)


def _ffn_kernel(x_ref, g_ref, wu_ref, wd_ref, gf_ref, o_ref, h_ref, acc_ref, *, final_norm):
    j = pl.program_id(1)

    @pl.when(j == 0)
    def _():
        h_ref[...] = _rms(x_ref[...], g_ref[...]).astype(BF16)
        acc_ref[...] = x_ref[...]

    f = jnp.maximum(_dot(h_ref[...], wu_ref[...]), 0.0)
    acc_ref[...] += _dot((f * f).astype(BF16), wd_ref[...])

    @pl.when(j == pl.num_programs(1) - 1)
    def _():
        y = acc_ref[...]
        o_ref[...] = _rms(y, gf_ref[...]) if final_norm else y


def _ffn(x2d, g, wu, wd, gf, tm, tf, final_norm):
    m = x2d.shape[0]
    return pl.pallas_call(
        functools.partial(_ffn_kernel, final_norm=final_norm),
        grid=(m // tm, D_FF // tf),
        in_specs=[pl.BlockSpec((tm, D_MODEL), lambda i, j: (i, 0)),
                  pl.BlockSpec((1, D_MODEL), lambda i, j: (0, 0)),
                  pl.BlockSpec((D_MODEL, tf), lambda i, j: (0, j)),
                  pl.BlockSpec((tf, D_MODEL), lambda i, j: (j, 0)),
                  pl.BlockSpec((1, D_MODEL), lambda i, j: (0, 0))],
        out_specs=pl.BlockSpec((tm, D_MODEL), lambda i, j: (i, 0)),
        out_shape=jax.ShapeDtypeStruct((m, D_MODEL), F32),
        scratch_shapes=[pltpu.VMEM((tm, D_MODEL), BF16), pltpu.VMEM((tm, D_MODEL), F32)],
        compiler_params=_cparams(("parallel", "arbitrary")),
        name="ffn",
    )(x2d, g, wu, wd, gf)


def _prep_layer(l, norm1_g, w_in, b_gate, ln_v_g, ln_v_b, w_spatial, b_spatial, w_a_out, conv_w, conv_b,
                dt_bias, a_log, d_skip, ssm_norm_g, w_b_out, w_c_out, w_o, norm2_g, w_up, w_down, t_len):
    w = w_in[l]
    o_dt = 2 * D_A + D_B + CONV_DIM
    o_q = o_dt + SSM_HEADS
    o_gate = o_q + 3 * D_C
    w_perm = jnp.concatenate(
        [w[:, :o_dt], w[:, o_gate:], w[:, o_q:o_gate], w[:, o_dt:o_q],
         jnp.zeros((D_MODEL, DT_W - SSM_HEADS), w.dtype)], axis=1).astype(BF16)
    tril = jnp.tril(jnp.ones((A_CHUNK, A_CHUNK), bool))
    w_tril = jnp.where(tril[None], w_spatial[l], 0)
    bs_exp = jnp.repeat(b_spatial[l].T, A_GROUP_DIM, axis=1)
    w_exp = jnp.repeat(jnp.transpose(w_tril[:, :t_len, :t_len], (2, 1, 0)), A_GROUP_DIM, axis=2)
    pad_h = lambda v: jnp.concatenate([v, jnp.zeros((LANES - SSM_HEADS,), v.dtype)])[None, :]
    return dict(
        norm1_g=norm1_g[l][None, :], w_in=w_perm, b_gate=b_gate[l][None, :],
        ln_g=ln_v_g[l][None, :], ln_b=ln_v_b[l][None, :],
        w_tril=w_tril.astype(BF16), bs_exp=bs_exp, w_exp=w_exp, bs_exp_s=bs_exp[:t_len],
        conv_w=conv_w[l], conv_b=conv_b[l][None, :], dt_bias=pad_h(dt_bias[l]), a_log=pad_h(a_log[l]),
        d_skip=jnp.repeat(d_skip[l], SSM_HEAD_DIM)[None, :], ssm_norm_g=ssm_norm_g[l][None, :],
        w_a_out=w_a_out[l].astype(BF16), w_b_out=w_b_out[l].astype(BF16), w_c_out=w_c_out[l].astype(BF16),
        w_o=w_o[l].astype(BF16), norm2_g=norm2_g[l][None, :],
        w_up=w_up[l].astype(BF16), w_down=w_down[l].astype(BF16))


def _rope_tables(pos):
    inv = jnp.power(jnp.float32(ROPE_THETA), -jnp.arange(0, ATT_HEAD_DIM, 2, dtype=F32) / ATT_HEAD_DIM)
    ang = pos.astype(F32)[:, None] * inv[None, :]
    cos, sin = jnp.cos(ang), jnp.sin(ang)
    cos_h = jnp.concatenate([cos, cos], axis=1)
    sin_h = jnp.concatenate([-sin, sin], axis=1)
    return jnp.tile(cos_h, (1, ATT_HEADS)), jnp.tile(sin_h, (1, ATT_HEADS))


def kernel(x_prompt, x_sample, cache_k, cache_v, state_ssm, state_conv, page_table, norm1_g, w_in, b_gate, ln_v_g, ln_v_b, w_spatial, b_spatial, w_a_out, conv_w, conv_b, dt_bias, a_log, d_skip, ssm_norm_g, w_b_out, w_c_out, w_o, norm2_g, w_up, w_down, norm_f_g):
    bp, s_len, _ = x_prompt.shape
    db, t_len, _ = x_sample.shape
    depth = w_in.shape[0]
    past_len = page_table.shape[1] * PAGE_SIZE
    mp, ms = bp * s_len, db * t_len
    cos_p, sin_p = _rope_tables(jnp.arange(s_len))
    cos_s, sin_s = _rope_tables(past_len + jnp.arange(t_len))
    cos_s, sin_s = jnp.tile(cos_s, (db, 1)), jnp.tile(sin_s, (db, 1))
    conv_zero = jnp.zeros((1, bp, SSM_CONV - 1, CONV_DIM), F32)
    h_zero = jnp.zeros((1, bp, D_B, SSM_STATE), F32)
    gf = norm_f_g[None, :]
    tm_p = 1024
    cache_kt = jnp.transpose(cache_k, (0, 1, 3, 4, 2))
    cache_vt = jnp.transpose(cache_v, (0, 1, 3, 4, 2))

    xp = x_prompt.reshape(mp, D_MODEL)
    xs = x_sample.reshape(ms, D_MODEL)
    outs = {k: [] for k in ("a_v_s", "ssm_p", "ssm_s", "conv_p", "conv_s", "k_p", "v_p", "k_s", "v_s")}
    for l in range(depth):
        w = _prep_layer(l, norm1_g, w_in, b_gate, ln_v_g, ln_v_b, w_spatial, b_spatial, w_a_out, conv_w, conv_b,
                        dt_bias, a_log, d_skip, ssm_norm_g, w_b_out, w_c_out, w_o, norm2_g, w_up, w_down, t_len)
        last = l == depth - 1

        proj, v_att = _in_proj(xp, w["norm1_g"], w["w_in"], tm_p)
        ya = _branch_a_prompt(proj, w["ln_g"], w["ln_b"], w["w_tril"], w["bs_exp"])
        yb, h_new = _branch_b(proj, bp, SSM_CHUNK, conv_zero, h_zero, 0, w["conv_w"], w["conv_b"], w["dt_bias"],
                              w["a_log"], w["d_skip"], w["ssm_norm_g"])
        q_rot, k_rot, sel_bias = _rope_prompt(proj, bp, cos_p, sin_p)
        yc = _moba_prompt(q_rot, k_rot, v_att, sel_bias, bp)
        x1 = _merge(xp, ya, yb, yc, proj, w["b_gate"], w["w_a_out"], w["w_b_out"], w["w_c_out"], w["w_o"], 512)
        xp = _ffn(x1, w["norm2_g"], w["w_up"], w["w_down"], gf, tm_p, 1024, last)
        proj3 = proj.reshape(bp, s_len, N_COLS)
        outs["ssm_p"].append(h_new.reshape(bp, SSM_HEADS, SSM_HEAD_DIM, SSM_STATE))
        outs["conv_p"].append(proj3[:, s_len - (SSM_CONV - 1):, C_XBC:C_XBC + CONV_DIM])
        outs["k_p"].append(k_rot.reshape(bp, s_len, ATT_HEADS, ATT_HEAD_DIM))
        outs["v_p"].append(v_att.reshape(bp, s_len, ATT_HEADS, ATT_HEAD_DIM))

        proj, v_att = _in_proj(xs, w["norm1_g"], w["w_in"], ms)
        ya, vn = _branch_a_sample(proj, t_len, w["ln_g"], w["ln_b"], w["w_exp"], w["bs_exp_s"])
        yb, h_new = _branch_b(proj, db, t_len, state_conv, state_ssm.reshape(depth, db, D_B, SSM_STATE), l,
                              w["conv_w"], w["conv_b"], w["dt_bias"], w["a_log"], w["d_skip"], w["ssm_norm_g"])
        q_rot, k_rot = _rope_sample(proj, cos_s, sin_s)
        proj3 = proj.reshape(db, t_len, N_COLS)
        v_new = v_att.reshape(db, t_len, ATT_HEADS, ATT_HEAD_DIM)
        k_new = k_rot.reshape(db, t_len, ATT_HEADS, ATT_HEAD_DIM)
        q_htd = jnp.transpose(q_rot.reshape(db, t_len, ATT_HEADS, ATT_HEAD_DIM), (0, 2, 1, 3))
        o_htd = _moba_sample(q_htd, k_new, v_new, cache_kt, cache_vt, page_table, l)
        yc = jnp.transpose(o_htd, (0, 2, 1, 3)).reshape(ms, D_C).astype(BF16)
        x1 = _merge(xs, ya, yb, yc, proj, w["b_gate"], w["w_a_out"], w["w_b_out"], w["w_c_out"], w["w_o"], ms)
        xs = _ffn(x1, w["norm2_g"], w["w_up"], w["w_down"], gf, ms, 1024, last)
        outs["a_v_s"].append(vn.reshape(db, t_len, D_A))
        outs["ssm_s"].append(h_new.reshape(db, SSM_HEADS, SSM_HEAD_DIM, SSM_STATE))
        if t_len >= SSM_CONV - 1:
            conv_new = proj3[:, t_len - (SSM_CONV - 1):, C_XBC:C_XBC + CONV_DIM]
        else:
            conv_new = jnp.concatenate([state_conv[l], proj3[:, :, C_XBC:C_XBC + CONV_DIM]],
                                       axis=1)[:, -(SSM_CONV - 1):]
        outs["conv_s"].append(conv_new)
        outs["k_s"].append(k_new)
        outs["v_s"].append(v_new)

    st = lambda k: jnp.stack(outs[k])
    return (xp.reshape(bp, s_len, D_MODEL), xs.reshape(db, t_len, D_MODEL), st("a_v_s"), st("ssm_p"), st("ssm_s"),
            st("conv_p"), st("conv_s"), st("k_p"), st("v_p"), st("k_s"), st("v_s"))
```

```python
import functools
import math

import jax
import jax.numpy as jnp
from jax import lax
from jax.experimental import pallas as pl
from jax.experimental.pallas import tpu as pltpu

F32 = jnp.float32
BF16 = jnp.bfloat16
NEG_INF = float("-inf")

D_MODEL = 1024
PAGE_SIZE = 128
A_CHUNK = 128
D_A = D_MODEL // 2
A_GROUPS = 8
A_GROUP_DIM = D_A // A_GROUPS
D_B = D_MODEL
SSM_HEAD_DIM = 64
SSM_HEADS = D_B // SSM_HEAD_DIM
SSM_GROUPS = 4
SSM_STATE = 128
SSM_CONV = 4
SSM_CHUNK = 128
CONV_DIM = D_B + 2 * SSM_GROUPS * SSM_STATE
ATT_HEADS = 8
ATT_HEAD_DIM = 64
D_C = ATT_HEADS * ATT_HEAD_DIM
MOBA_BLOCK = 256
MOBA_TOPK = 3
ROPE_THETA = 10000.0
N_BRANCH = 3
D_FF = 4 * D_MODEL
EPS = 1e-6

LANES = 128
C_U, C_V, C_Z, C_XBC, C_GATE, C_Q, C_K, C_VATT, C_DT = 0, 512, 1024, 2048, 4096, 7168, 7680, 8192, 8704
DT_W = 256
N_COLS = C_DT + DT_W
IN_TN = 1280
VMEM_LIMIT = 56 * 1024 * 1024


def _cparams(sem):
    return pltpu.CompilerParams(dimension_semantics=sem, vmem_limit_bytes=VMEM_LIMIT)


def _dot(a, b):
    return jnp.dot(a, b, preferred_element_type=F32)


def _dot_nt(a, b):
    return lax.dot_general(a, b, (((1,), (1,)), ((), ())), preferred_element_type=F32)


def _split3(x):
    x1 = x.astype(BF16)
    r = x - x1.astype(F32)
    x2 = r.astype(BF16)
    r = r - x2.astype(F32)
    return x1, x2, r.astype(BF16)


def _dot_hi_nt(a, b):
    a1, a2, a3 = _split3(a)
    b1, b2, b3 = _split3(b)
    return (_dot_nt(a1, b1) + (_dot_nt(a1, b2) + _dot_nt(a2, b1))
            + (_dot_nt(a1, b3) + _dot_nt(a2, b2) + _dot_nt(a3, b1)))


def _rms(x, g):
    return x * lax.rsqrt(jnp.mean(x * x, axis=-1, keepdims=True) + EPS) * g


def _gelu(x):
    return 0.5 * x * (1.0 + lax.erf(x * math.sqrt(0.5)))


def _silu(x):
    return x * jax.nn.sigmoid(x)


def _inproj_kernel(x_ref, g_ref, w_ref, o_ref, v_ref, h_ref):
    j = pl.program_id(1)

    @pl.when(j == 0)
    def _():
        h_ref[...] = _rms(x_ref[...], g_ref[...]).astype(BF16)

    o_ref[...] = _dot(h_ref[...], w_ref[...])

    @pl.when(j == C_VATT // IN_TN)
    def _():
        v_ref[...] = o_ref[:, C_VATT % IN_TN:C_VATT % IN_TN + D_C]


def _in_proj(x2d, g, w_bf, tm):
    m = x2d.shape[0]
    assert C_VATT // IN_TN == (C_VATT + D_C - 1) // IN_TN
    return pl.pallas_call(
        _inproj_kernel,
        grid=(m // tm, N_COLS // IN_TN),
        in_specs=[pl.BlockSpec((tm, D_MODEL), lambda i, j: (i, 0)),
                  pl.BlockSpec((1, D_MODEL), lambda i, j: (0, 0)),
                  pl.BlockSpec((D_MODEL, IN_TN), lambda i, j: (0, j))],
        out_specs=[pl.BlockSpec((tm, IN_TN), lambda i, j: (i, j)),
                   pl.BlockSpec((tm, D_C), lambda i, j: (i, 0))],
        out_shape=[jax.ShapeDtypeStruct((m, N_COLS), F32), jax.ShapeDtypeStruct((m, D_C), F32)],
        scratch_shapes=[pltpu.VMEM((tm, D_MODEL), BF16)],
        compiler_params=_cparams(("parallel", "arbitrary")),
        name="in_proj",
    )(x2d, g, w_bf)


def _layernorm(x, g, b):
    xc = x - jnp.mean(x, axis=-1, keepdims=True)
    return xc * lax.rsqrt(jnp.mean(xc * xc, axis=-1, keepdims=True) + EPS) * g + b


A_ROWS = 4 * A_CHUNK


def _branch_a_kernel(u_ref, v_ref, g_ref, b_ref, w_ref, bs_ref, ya_ref):
    first = lax.broadcasted_iota(jnp.int32, (A_CHUNK, LANES), 1) < A_GROUP_DIM
    for c in range(A_ROWS // A_CHUNK):
        rows = slice(c * A_CHUNK, (c + 1) * A_CHUNK)
        gu = _gelu(u_ref[rows, :])
        vn = _layernorm(_gelu(v_ref[rows, :]), g_ref[...], b_ref[...]).astype(BF16)
        parts = []
        for p in range(A_GROUPS // 2):
            vp = vn[:, p * LANES:(p + 1) * LANES]
            parts.append(jnp.where(first, _dot(w_ref[2 * p], vp), _dot(w_ref[2 * p + 1], vp)))
        mixed = jnp.concatenate(parts, axis=1) + bs_ref[...]
        ya_ref[rows, :] = (gu * mixed).astype(BF16)


def _branch_a_prompt(proj, ln_g, ln_b, w_tril_bf, bs_exp):
    m = proj.shape[0]
    return pl.pallas_call(
        _branch_a_kernel,
        grid=(m // A_ROWS,),
        in_specs=[pl.BlockSpec((A_ROWS, D_A), lambda i: (i, C_U // D_A)),
                  pl.BlockSpec((A_ROWS, D_A), lambda i: (i, C_V // D_A)),
                  pl.BlockSpec((1, D_A), lambda i: (0, 0)),
                  pl.BlockSpec((1, D_A), lambda i: (0, 0)),
                  pl.BlockSpec((A_GROUPS, A_CHUNK, A_CHUNK), lambda i: (0, 0, 0)),
                  pl.BlockSpec((A_CHUNK, D_A), lambda i: (0, 0))],
        out_specs=pl.BlockSpec((A_ROWS, D_A), lambda i: (i, 0)),
        out_shape=jax.ShapeDtypeStruct((m, D_A), BF16),
        compiler_params=_cparams(("parallel",)),
        name="branch_a_prompt",
    )(proj, proj, ln_g, ln_b, w_tril_bf, bs_exp)


def _branch_a_sample_kernel(u_ref, v_ref, g_ref, b_ref, wexp_ref, bs_ref, ya_ref, vn_ref):
    gu = _gelu(u_ref[...])
    vn = _layernorm(_gelu(v_ref[...]), g_ref[...], b_ref[...])
    vn_ref[...] = vn
    mixed = bs_ref[...]
    for s in range(vn.shape[0]):
        mixed = mixed + wexp_ref[s] * vn[s:s + 1, :]
    ya_ref[...] = (gu * mixed).astype(BF16)


def _branch_a_sample(proj, t_len, ln_g, ln_b, w_exp, bs_exp):
    m = proj.shape[0]
    return pl.pallas_call(
        _branch_a_sample_kernel,
        grid=(m // t_len,),
        in_specs=[pl.BlockSpec((t_len, D_A), lambda i: (i, C_U // D_A)),
                  pl.BlockSpec((t_len, D_A), lambda i: (i, C_V // D_A)),
                  pl.BlockSpec((1, D_A), lambda i: (0, 0)),
                  pl.BlockSpec((1, D_A), lambda i: (0, 0)),
                  pl.BlockSpec((t_len, t_len, D_A), lambda i: (0, 0, 0)),
                  pl.BlockSpec((t_len, D_A), lambda i: (0, 0))],
        out_specs=[pl.BlockSpec((t_len, D_A), lambda i: (i, 0)),
                   pl.BlockSpec((t_len, D_A), lambda i: (i, 0))],
        out_shape=[jax.ShapeDtypeStruct((m, D_A), BF16), jax.ShapeDtypeStruct((m, D_A), F32)],
        compiler_params=_cparams(("parallel",)),
        name="branch_a_sample",
    )(proj, proj, ln_g, ln_b, w_exp, bs_exp)


XP_OFF = 8


def _ssd_kernel(z_ref, xbc_ref, dt_ref, cprev_ref, h0_ref, cw_ref, cb_ref, dtb_ref, alog_ref, dsk_ref, ng_ref,
                yb_ref, hout_ref, xp_ref, *, rows):
    t_len = SSM_CHUNK
    c = pl.program_id(1)
    xbc, z, dtraw = xbc_ref[...], z_ref[...], dt_ref[...][:, :LANES]
    if rows < t_len:
        def pad(a):
            return jnp.concatenate([a, jnp.zeros((t_len - rows, a.shape[1]), a.dtype)], axis=0)
        xbc, z, dtraw = pad(xbc), pad(z), pad(dtraw)

    @pl.when(c == 0)
    def _():
        xp_ref[XP_OFF - 3:XP_OFF, :] = cprev_ref[...]
        hout_ref[...] = h0_ref[...]

    @pl.when(c > 0)
    def _():
        xp_ref[XP_OFF - 3:XP_OFF, :] = xp_ref[XP_OFF + t_len - 3:XP_OFF + t_len, :]

    xp_ref[XP_OFF:XP_OFF + t_len, :] = xbc
    acc = cb_ref[...] + xbc * cw_ref[SSM_CONV - 1:SSM_CONV, :]
    for tap in range(SSM_CONV - 1):
        lo = XP_OFF - (SSM_CONV - 1) + tap
        acc = acc + xp_ref[lo:lo + t_len, :] * cw_ref[tap:tap + 1, :]
    act = _silu(acc)
    xs = act[:, :D_B]
    bm = act[:, D_B:D_B + SSM_GROUPS * SSM_STATE].astype(BF16)
    cm = act[:, D_B + SSM_GROUPS * SSM_STATE:].astype(BF16)

    dt = jax.nn.softplus(dtraw + dtb_ref[...])
    row_i = lax.broadcasted_iota(jnp.int32, (t_len, t_len), 0)
    col_i = lax.broadcasted_iota(jnp.int32, (t_len, t_len), 1)
    if rows < t_len:
        dt = jnp.where(lax.broadcasted_iota(jnp.int32, dt.shape, 0) < rows, dt, 0.0)
    a = dt * (-jnp.exp(alog_ref[...]))
    tril = jnp.where(col_i <= row_i, 1.0, 0.0).astype(BF16)
    a1, a2, a3 = _split3(a)
    acs = _dot(tril, a1) + _dot(tril, a2) + _dot(tril, a3)
    acs_t = acs.T
    dt_t = dt.T
    xs_t = xs.T
    causal_t = row_i <= col_i
    rep = SSM_HEADS // SSM_GROUPS
    ys = []
    for g in range(SSM_GROUPS):
        bg = bm[:, g * SSM_STATE:(g + 1) * SSM_STATE]
        cg = cm[:, g * SSM_STATE:(g + 1) * SSM_STATE]
        cb_t = _dot_nt(bg, cg)
        for r in range(rep):
            h = g * rep + r
            hs = slice(h * SSM_HEAD_DIM, (h + 1) * SSM_HEAD_DIM)
            acs_row = acs_t[h:h + 1, :]
            acs_col = acs[:, h:h + 1]
            dec_t = jnp.exp(jnp.where(causal_t, acs_row - acs_col, NEG_INF))
            w_t = (cb_t * dec_t).astype(BF16)
            xdt_t = xs_t[hs, :] * dt_t[h:h + 1, :]
            h_in = hout_ref[hs, :]
            y_t = _dot(xdt_t.astype(BF16), w_t) + _dot_nt(h_in.astype(BF16), cg) * jnp.exp(acs_row)
            last = acs_row[:, t_len - 1:t_len]
            st = _dot((xdt_t * jnp.exp(last - acs_row)).astype(BF16), bg)
            hout_ref[hs, :] = jnp.exp(last) * h_in + st
            ys.append(y_t)
    y = jnp.concatenate(ys, axis=0).T + dsk_ref[...] * xs
    yb = _rms(y * _silu(z), ng_ref[...])
    yb_ref[...] = yb[:rows].astype(BF16)


def _branch_b(proj, n_seq, rows, conv_prev, h0, layer, conv_w, conv_b, dt_bias_p, a_log_p, d_skip_exp, norm_g):
    m = proj.shape[0]
    n_chunk = m // (n_seq * rows)
    const = lambda b, c: (0, 0)
    return pl.pallas_call(
        functools.partial(_ssd_kernel, rows=rows),
        grid=(n_seq, n_chunk),
        in_specs=[pl.BlockSpec((rows, D_B), lambda b, c: (b * n_chunk + c, C_Z // D_B)),
                  pl.BlockSpec((rows, CONV_DIM), lambda b, c: (b * n_chunk + c, C_XBC // CONV_DIM)),
                  pl.BlockSpec((rows, DT_W), lambda b, c: (b * n_chunk + c, C_DT // DT_W)),
                  pl.BlockSpec((None, None, SSM_CONV - 1, CONV_DIM), lambda b, c: (layer, b, 0, 0)),
                  pl.BlockSpec((None, None, D_B, SSM_STATE), lambda b, c: (layer, b, 0, 0)),
                  pl.BlockSpec((SSM_CONV, CONV_DIM), const),
                  pl.BlockSpec((1, CONV_DIM), const),
                  pl.BlockSpec((1, LANES), const),
                  pl.BlockSpec((1, LANES), const),
                  pl.BlockSpec((1, D_B), const),
                  pl.BlockSpec((1, D_B), const)],
        out_specs=[pl.BlockSpec((rows, D_B), lambda b, c: (b * n_chunk + c, 0)),
                   pl.BlockSpec((None, D_B, SSM_STATE), lambda b, c: (b, 0, 0))],
        out_shape=[jax.ShapeDtypeStruct((m, D_B), BF16),
                   jax.ShapeDtypeStruct((n_seq, D_B, SSM_STATE), F32)],
        scratch_shapes=[pltpu.VMEM((XP_OFF + SSM_CHUNK, CONV_DIM), F32)],
        compiler_params=_cparams(("parallel", "arbitrary")),
        name="branch_b_rows%d" % rows,
    )(proj, proj, proj, conv_prev, h0, conv_w, conv_b, dt_bias_p, a_log_p, d_skip_exp, norm_g)


def _rope_apply(x, cos, sin_signed, first):
    partner = jnp.where(first, pltpu.roll(x, D_C - ATT_HEAD_DIM // 2, 1), pltpu.roll(x, ATT_HEAD_DIM // 2, 1))
    return x * cos + partner * sin_signed


def _rope_kernel(q_ref, k_ref, cos_ref, sin_ref, qo_ref, ko_ref):
    lane = lax.broadcasted_iota(jnp.int32, q_ref.shape, 1)
    first = (lane % ATT_HEAD_DIM) < ATT_HEAD_DIM // 2
    cos, sin = cos_ref[...], sin_ref[...]
    qo_ref[...] = _rope_apply(q_ref[...], cos, sin, first)
    ko_ref[...] = _rope_apply(k_ref[...], cos, sin, first)


def _rope_gate_kernel(q_ref, k_ref, cos_ref, sin_ref, qo_ref, ko_ref, bias_ref, km_ref, *, nblk):
    j = pl.program_id(1)
    lane = lax.broadcasted_iota(jnp.int32, q_ref.shape, 1)
    first = (lane % ATT_HEAD_DIM) < ATT_HEAD_DIM // 2
    cos, sin = cos_ref[...], sin_ref[...]
    qr = _rope_apply(q_ref[...], cos, sin, first)
    kr = _rope_apply(k_ref[...], cos, sin, first)
    qo_ref[...] = qr
    ko_ref[...] = kr

    @pl.when(j == 0)
    def _():
        km_ref[...] = jnp.zeros(km_ref.shape, F32)

    km = km_ref[...]
    km_ref[pl.ds(j, 1), :] = jnp.mean(kr, axis=0, keepdims=True)
    klane = lax.broadcasted_iota(jnp.int32, km.shape, 1) // ATT_HEAD_DIM
    kstack = jnp.concatenate([jnp.where(klane == h, km, 0.0) for h in range(ATT_HEADS)], axis=0)
    gate = _dot_hi_nt(kstack, qr)
    bidx = lax.broadcasted_iota(jnp.int32, (nblk, MOBA_BLOCK), 0)
    past = bidx < j
    n_sel = max(1, min(MOBA_TOPK, nblk - 1))
    for h in range(ATT_HEADS):
        g = jnp.where(past, gate[h * nblk:(h + 1) * nblk, :], NEG_INF)
        sel = past & (_topk_rank(g, bidx, nblk - 1, 0) < n_sel)
        bias_ref[h * nblk:(h + 1) * nblk, :] = jnp.where(sel, 0.0, NEG_INF)


def _rope_prompt(proj, n_seq, cos, sin):
    m = proj.shape[0]
    nblk = m // (n_seq * MOBA_BLOCK)
    row = lambda b, j: (b * nblk + j, 0)
    return pl.pallas_call(
        functools.partial(_rope_gate_kernel, nblk=nblk),
        grid=(n_seq, nblk),
        in_specs=[pl.BlockSpec((MOBA_BLOCK, D_C), lambda b, j: (b * nblk + j, C_Q // D_C)),
                  pl.BlockSpec((MOBA_BLOCK, D_C), lambda b, j: (b * nblk + j, C_K // D_C)),
                  pl.BlockSpec((MOBA_BLOCK, D_C), lambda b, j: (j, 0)),
                  pl.BlockSpec((MOBA_BLOCK, D_C), lambda b, j: (j, 0))],
        out_specs=[pl.BlockSpec((MOBA_BLOCK, D_C), row),
                   pl.BlockSpec((MOBA_BLOCK, D_C), row),
                   pl.BlockSpec((None, ATT_HEADS * nblk, MOBA_BLOCK), lambda b, j: (b, 0, j))],
        out_shape=[jax.ShapeDtypeStruct((m, D_C), F32), jax.ShapeDtypeStruct((m, D_C), F32),
                   jax.ShapeDtypeStruct((n_seq, ATT_HEADS * nblk, m // n_seq), F32)],
        scratch_shapes=[pltpu.VMEM((nblk, D_C), F32)],
        compiler_params=_cparams(("parallel", "arbitrary")),
        name="rope_prompt",
    )(proj, proj, cos, sin)


def _rope_sample(proj, cos, sin):
    m = proj.shape[0]
    return pl.pallas_call(
        _rope_kernel,
        grid=(1,),
        in_specs=[pl.BlockSpec((m, D_C), lambda i: (0, C_Q // D_C)),
                  pl.BlockSpec((m, D_C), lambda i: (0, C_K // D_C)),
                  pl.BlockSpec((m, D_C), lambda i: (0, 0)),
                  pl.BlockSpec((m, D_C), lambda i: (0, 0))],
        out_specs=[pl.BlockSpec((m, D_C), lambda i: (0, 0)), pl.BlockSpec((m, D_C), lambda i: (0, 0))],
        out_shape=[jax.ShapeDtypeStruct((m, D_C), F32), jax.ShapeDtypeStruct((m, D_C), F32)],
        compiler_params=_cparams(("arbitrary",)),
        name="rope_sample",
    )(proj, proj, cos, sin)


def _topk_rank(gate, idx, n_cand, axis):
    rank = jnp.zeros(gate.shape, jnp.int32)
    for i in range(n_cand):
        gi = gate[:, i:i + 1] if axis == 1 else gate[i:i + 1, :]
        beats = (gi > gate) | ((gi == gate) & (i < idx))
        rank = rank + jnp.where(beats, 1, 0)
    return rank


SUBLANES = 8
ONES_ROWS = 2 * SUBLANES


def _fold_rows(x, op):
    return op(x.reshape(x.shape[0] // SUBLANES, SUBLANES, x.shape[1]), axis=0)


def _moba_prompt_kernel(q_ref, k_ref, v_ref, bias_ref, o_ref, kb_ref, vt_ref, *, nblk):
    qb = pl.program_id(2)
    n_head = LANES // ATT_HEAD_DIM

    @pl.when(qb == 0)
    def _():
        ones = jnp.ones((ONES_ROWS, MOBA_BLOCK), BF16)
        for j in range(nblk):
            kb_ref[j] = k_ref[j * MOBA_BLOCK:(j + 1) * MOBA_BLOCK, :].astype(BF16)
            vt_ref[j] = jnp.concatenate([v_ref[j * MOBA_BLOCK:(j + 1) * MOBA_BLOCK, :].T.astype(BF16), ones], axis=0)

    q = q_ref[...] * (ATT_HEAD_DIM ** -0.5 * math.log2(math.e))
    lane = lax.broadcasted_iota(jnp.int32, (MOBA_BLOCK, LANES), 1)
    own_bias = jnp.where(lax.broadcasted_iota(jnp.int32, (MOBA_BLOCK, MOBA_BLOCK), 0)
                         <= lax.broadcasted_iota(jnp.int32, (MOBA_BLOCK, MOBA_BLOCK), 1), 0.0, NEG_INF)
    row = lax.broadcasted_iota(jnp.int32, (LANES, MOBA_BLOCK), 0)
    qms = [jnp.where((lane >= h * ATT_HEAD_DIM) & (lane < (h + 1) * ATT_HEAD_DIM), q, 0.0).astype(BF16)
           for h in range(n_head)]

    def attend(n_past):
        n_all = n_past + 1

        def score_block(h, j):
            st = _dot_nt(kb_ref[j], qms[h])
            if j == n_past:
                st = st + own_bias
                return st, None, _fold_rows(st, jnp.max)
            bias = bias_ref[h * nblk + j:h * nblk + j + 1, :]
            return st, bias, _fold_rows(st, jnp.max) + bias

        def fold_max(blocks):
            mx = blocks[0][2]
            for blk in blocks[1:]:
                mx = jnp.maximum(mx, blk[2])
            return jnp.max(mx, axis=0, keepdims=True)

        def weigh(blk, m, j, acc):
            st, bias, _ = blk
            p = jnp.exp2(st - (m if bias is None else m - bias))
            pv = _dot(vt_ref[j], p.astype(BF16))
            return pv if acc is None else acc + pv

        accs = []
        for h in range(n_head):
            blocks = [score_block(h, j) for j in range(n_all)]
            m = fold_max(blocks)
            acc = None
            for j in range(n_all):
                acc = weigh(blocks[j], m, j, acc)
            accs.append(acc)
        outs = [acc[:LANES] / acc[LANES:LANES + 1] for acc in accs]
        o_ref[...] = jnp.where(row < ATT_HEAD_DIM, outs[0], outs[1]).T.astype(BF16)

    for n_past in range(nblk):
        pl.when(qb == n_past)(functools.partial(attend, n_past))


def _moba_prompt(q_rot, k_rot, v_att, bias, n_seq):
    m = q_rot.shape[0]
    s_len = m // n_seq
    nblk = s_len // MOBA_BLOCK
    n_pair = D_C // LANES
    n_head = LANES // ATT_HEAD_DIM
    return pl.pallas_call(
        functools.partial(_moba_prompt_kernel, nblk=nblk),
        grid=(n_seq, n_pair, nblk),
        in_specs=[pl.BlockSpec((MOBA_BLOCK, LANES), lambda b, hp, i: (b * nblk + i, hp)),
                  pl.BlockSpec((s_len, LANES), lambda b, hp, i: (b, hp)),
                  pl.BlockSpec((s_len, LANES), lambda b, hp, i: (b, hp)),
                  pl.BlockSpec((None, n_head * nblk, MOBA_BLOCK), lambda b, hp, i: (b, hp, i))],
        out_specs=pl.BlockSpec((MOBA_BLOCK, LANES), lambda b, hp, i: (b * nblk + i, hp)),
        out_shape=jax.ShapeDtypeStruct((m, D_C), BF16),
        scratch_shapes=[pltpu.VMEM((nblk, MOBA_BLOCK, LANES), BF16),
                        pltpu.VMEM((nblk, LANES + ONES_ROWS, MOBA_BLOCK), BF16)],
        compiler_params=_cparams(("parallel", "parallel", "arbitrary")),
        name="moba_prompt",
    )(q_rot, k_rot, v_att, bias)


SAMPLE_PAGE_BUFFERS = 32
SAMPLE_PAGE_UNROLL = 8


def _moba_sample_kernel(pt_ref, q_ref, kn_ref, vn_ref, ck_hbm, cv_hbm, o_ref,
                        buf_ref, sem, s_ref, own_ref, oacc_ref, *, n_seq, n_pages, t_len, layer):
    nbuf = SAMPLE_PAGE_BUFFERS
    per_seq = 2 * n_pages
    total = n_seq * per_seq
    scale = ATT_HEAD_DIM ** -0.5
    ppb = MOBA_BLOCK // PAGE_SIZE
    n_blk = n_pages // ppb
    n_row = ATT_HEADS * t_len
    rows = lambda h: slice(h * t_len, (h + 1) * t_len)

    def page_copy(src_hbm, page, slot):
        return pltpu.make_async_copy(src_hbm.at[page, layer], buf_ref.at[slot], sem.at[slot])

    def start_fetch(n, slot):
        b = n // per_seq
        i = n % per_seq

        @pl.when(i < n_pages)
        def _():
            page_copy(ck_hbm, pt_ref[b, i], slot).start()

        @pl.when(i >= n_pages)
        def _():
            page_copy(cv_hbm, pt_ref[b, i - n_pages], slot).start()

    def finish_fetch(slot):
        page_copy(ck_hbm, 0, slot).wait()

    def refill(n, slot):
        @pl.when(n + nbuf < total)
        def _():
            start_fetch(n + nbuf, slot)

    b = pl.program_id(0)

    @pl.when(b == 0)
    def _():
        for n in range(nbuf):
            start_fetch(n, n)

    def one_sequence():
        base = b * per_seq

        grp = SAMPLE_PAGE_UNROLL

        def score_pages(g, c):
            pages = [g * grp + u for u in range(grp)]
            for i in pages:
                finish_fetch(i % nbuf)
            for i in pages:
                for h in range(ATT_HEADS):
                    sc = _dot(q_ref[h].astype(BF16), buf_ref[i % nbuf, h].astype(BF16))
                    s_ref[i, rows(h), :] = sc * scale
            for i in pages:
                refill(base + i, i % nbuf)
            return c

        lax.fori_loop(0, n_pages // grp, score_pages, 0)
        select_and_normalise(q_ref, kn_ref, vn_ref)

        def weigh_pages(g, accs):
            pages = [g * grp + u for u in range(grp)]
            for i in pages:
                finish_fetch((n_pages + i) % nbuf)
            accs = tuple(
                accs[h] + _dot_nt(
                    jnp.concatenate([s_ref[i, rows(h), :] for i in pages], axis=1).astype(BF16),
                    jnp.concatenate([buf_ref[(n_pages + i) % nbuf, h] for i in pages], axis=1).astype(BF16))
                for h in range(ATT_HEADS))
            for i in pages:
                refill(base + n_pages + i, (n_pages + i) % nbuf)
            return accs

        accs = lax.fori_loop(0, n_pages // grp, weigh_pages, tuple(oacc_ref[h] for h in range(ATT_HEADS)))
        for h in range(ATT_HEADS):
            o_ref[h] = accs[h] * own_ref[rows(h), 0:1]

    def select_and_normalise(q_ref, kn_ref, vn_ref):
        lane = lax.broadcasted_iota(jnp.int32, (n_row, LANES), 1)
        gate = jnp.full((n_row, LANES), NEG_INF, F32)
        for j in range(n_blk):
            tot = s_ref[ppb * j]
            for i in range(1, ppb):
                tot = tot + s_ref[ppb * j + i]
            gate = jnp.where(lane == j, jnp.sum(tot, axis=1, keepdims=True), gate)
        sel = jnp.where(_topk_rank(gate, lane, n_blk, 1) < min(MOBA_TOPK, n_blk), 1, 0)
        zpad = jnp.zeros((PAGE_SIZE - t_len, ATT_HEAD_DIM), F32)
        for h in range(ATT_HEADS):
            kh = jnp.concatenate([kn_ref[:, h, :], zpad], axis=0).astype(BF16)
            own_ref[rows(h), :] = _dot_nt(q_ref[h].astype(BF16), kh) * scale
        tok = lax.broadcasted_iota(jnp.int32, (n_row, LANES), 0) % t_len
        so = jnp.where(lane <= tok, own_ref[...], NEG_INF)
        mv = so
        for j in range(n_blk):
            for i in range(ppb):
                mv = jnp.maximum(mv, jnp.where(sel[:, j:j + 1] > 0, s_ref[ppb * j + i], NEG_INF))
        m = jnp.max(mv, axis=1, keepdims=True)
        po = jnp.exp(so - m)
        own_ref[...] = po
        lv = po
        for j in range(n_blk):
            for i in range(ppb):
                pj = jnp.exp(jnp.where(sel[:, j:j + 1] > 0, s_ref[ppb * j + i], NEG_INF) - m)
                s_ref[ppb * j + i] = pj
                lv = lv + pj
        inv = 1.0 / jnp.sum(lv, axis=1, keepdims=True)
        for h in range(ATT_HEADS):
            vh = jnp.concatenate([vn_ref[:, h, :], zpad], axis=0).astype(BF16)
            oacc_ref[h] = _dot(own_ref[rows(h), :].astype(BF16), vh)
        own_ref[:, 0:1] = inv

    one_sequence()


def _moba_sample(q_htd, k_thd, v_thd, cache_kt, cache_vt, page_table, layer):
    n_seq, _, t_len, _ = q_htd.shape
    n_pages = page_table.shape[1]
    assert (2 * n_pages) % SAMPLE_PAGE_BUFFERS == 0 and n_pages % SAMPLE_PAGE_UNROLL == 0
    tok_shape = (n_seq, ATT_HEADS, t_len, ATT_HEAD_DIM)
    per_seq = lambda shape: pl.BlockSpec((None,) + shape, lambda b, pt: (b, 0, 0, 0))
    tok_blk = per_seq((ATT_HEADS, t_len, ATT_HEAD_DIM))
    new_blk = per_seq((t_len, ATT_HEADS, ATT_HEAD_DIM))
    grid_spec = pltpu.PrefetchScalarGridSpec(
        num_scalar_prefetch=1,
        grid=(n_seq,),
        in_specs=[tok_blk, new_blk, new_blk, pl.BlockSpec(memory_space=pl.ANY), pl.BlockSpec(memory_space=pl.ANY)],
        out_specs=tok_blk,
        scratch_shapes=[pltpu.VMEM((SAMPLE_PAGE_BUFFERS, ATT_HEADS, ATT_HEAD_DIM, PAGE_SIZE), F32),
                        pltpu.SemaphoreType.DMA((SAMPLE_PAGE_BUFFERS,)),
                        pltpu.VMEM((n_pages, ATT_HEADS * t_len, PAGE_SIZE), F32),
                        pltpu.VMEM((ATT_HEADS * t_len, PAGE_SIZE), F32),
                        pltpu.VMEM((ATT_HEADS, t_len, ATT_HEAD_DIM), F32)])
    return pl.pallas_call(
        functools.partial(_moba_sample_kernel, n_seq=n_seq, n_pages=n_pages, t_len=t_len, layer=layer),
        grid_spec=grid_spec,
        out_shape=jax.ShapeDtypeStruct(tok_shape, F32),
        compiler_params=_cparams(("arbitrary",)),
        name="moba_sample",
    )(page_table, q_htd, k_thd, v_thd, cache_kt, cache_vt)


def _merge_kernel(x_ref, ya_ref, yb_ref, yc_ref, g0_ref, g1_ref, g2_ref, bg_ref, wa_ref, wb_ref, wc_ref, wo_ref, o_ref):
    bg = bg_ref[...]
    merged = (jax.nn.sigmoid(g0_ref[...] + bg[:, :D_MODEL]) * _dot(ya_ref[...], wa_ref[...])
              + jax.nn.sigmoid(g1_ref[...] + bg[:, D_MODEL:2 * D_MODEL]) * _dot(yb_ref[...], wb_ref[...])
              + jax.nn.sigmoid(g2_ref[...] + bg[:, 2 * D_MODEL:]) * _dot(yc_ref[...], wc_ref[...]))
    o_ref[...] = x_ref[...] + _dot(merged.astype(BF16), wo_ref[...])


def _merge(x2d, ya, yb, yc, proj, b_gate, wa, wb, wc, wo, tm):
    m = x2d.shape[0]
    const = lambda i: (0, 0)
    gate_spec = lambda k: pl.BlockSpec((tm, D_MODEL), lambda i: (i, C_GATE // D_MODEL + k))
    return pl.pallas_call(
        _merge_kernel,
        grid=(m // tm,),
        in_specs=[pl.BlockSpec((tm, D_MODEL), lambda i: (i, 0)),
                  pl.BlockSpec((tm, D_A), lambda i: (i, 0)),
                  pl.BlockSpec((tm, D_B), lambda i: (i, 0)),
                  pl.BlockSpec((tm, D_C), lambda i: (i, 0)),
                  gate_spec(0), gate_spec(1), gate_spec(2),
                  pl.BlockSpec((1, N_BRANCH * D_MODEL), const),
                  pl.BlockSpec((D_A, D_MODEL), const),
                  pl.BlockSpec((D_B, D_MODEL), const),
                  pl.BlockSpec((D_C, D_MODEL), const),
                  pl.BlockSpec((D_MODEL, D_MODEL), const)],
        out_specs=pl.BlockSpec((tm, D_MODEL), lambda i: (i, 0)),
        out_shape=jax.ShapeDtypeStruct((m, D_MODEL), F32),
        compiler_params=_cparams(("parallel",)),
        name="merge",
    )(x2d, ya, yb, yc, proj, proj, proj, b_gate, wa, wb, wc, wo)


def _ffn_kernel(x_ref, g_ref, wu_ref, wd_ref, gf_ref, o_ref, h_ref, acc_ref, *, final_norm):
    j = pl.program_id(1)

    @pl.when(j == 0)
    def _():
        h_ref[...] = _rms(x_ref[...], g_ref[...]).astype(BF16)
        acc_ref[...] = x_ref[...]

    f = jnp.maximum(_dot(h_ref[...], wu_ref[...]), 0.0)
    acc_ref[...] += _dot((f * f).astype(BF16), wd_ref[...])

    @pl.when(j == pl.num_programs(1) - 1)
    def _():
        y = acc_ref[...]
        o_ref[...] = _rms(y, gf_ref[...]) if final_norm else y


def _ffn(x2d, g, wu, wd, gf, tm, tf, final_norm):
    m = x2d.shape[0]
    return pl.pallas_call(
        functools.partial(_ffn_kernel, final_norm=final_norm),
        grid=(m // tm, D_FF // tf),
        in_specs=[pl.BlockSpec((tm, D_MODEL), lambda i, j: (i, 0)),
                  pl.BlockSpec((1, D_MODEL), lambda i, j: (0, 0)),
                  pl.BlockSpec((D_MODEL, tf), lambda i, j: (0, j)),
                  pl.BlockSpec((tf, D_MODEL), lambda i, j: (j, 0)),
                  pl.BlockSpec((1, D_MODEL), lambda i, j: (0, 0))],
        out_specs=pl.BlockSpec((tm, D_MODEL), lambda i, j: (i, 0)),
        out_shape=jax.ShapeDtypeStruct((m, D_MODEL), F32),
        scratch_shapes=[pltpu.VMEM((tm, D_MODEL), BF16), pltpu.VMEM((tm, D_MODEL), F32)],
        compiler_params=_cparams(("parallel", "arbitrary")),
        name="ffn",
    )(x2d, g, wu, wd, gf)


def _prep_layer(l, norm1_g, w_in, b_gate, ln_v_g, ln_v_b, w_spatial, b_spatial, w_a_out, conv_w, conv_b,
                dt_bias, a_log, d_skip, ssm_norm_g, w_b_out, w_c_out, w_o, norm2_g, w_up, w_down, t_len):
    w = w_in[l]
    o_dt = 2 * D_A + D_B + CONV_DIM
    o_q = o_dt + SSM_HEADS
    o_gate = o_q + 3 * D_C
    w_perm = jnp.concatenate(
        [w[:, :o_dt], w[:, o_gate:], w[:, o_q:o_gate], w[:, o_dt:o_q],
         jnp.zeros((D_MODEL, DT_W - SSM_HEADS), w.dtype)], axis=1).astype(BF16)
    tril = jnp.tril(jnp.ones((A_CHUNK, A_CHUNK), bool))
    w_tril = jnp.where(tril[None], w_spatial[l], 0)
    bs_exp = jnp.repeat(b_spatial[l].T, A_GROUP_DIM, axis=1)
    w_exp = jnp.repeat(jnp.transpose(w_tril[:, :t_len, :t_len], (2, 1, 0)), A_GROUP_DIM, axis=2)
    pad_h = lambda v: jnp.concatenate([v, jnp.zeros((LANES - SSM_HEADS,), v.dtype)])[None, :]
    return dict(
        norm1_g=norm1_g[l][None, :], w_in=w_perm, b_gate=b_gate[l][None, :],
        ln_g=ln_v_g[l][None, :], ln_b=ln_v_b[l][None, :],
        w_tril=w_tril.astype(BF16), bs_exp=bs_exp, w_exp=w_exp, bs_exp_s=bs_exp[:t_len],
        conv_w=conv_w[l], conv_b=conv_b[l][None, :], dt_bias=pad_h(dt_bias[l]), a_log=pad_h(a_log[l]),
        d_skip=jnp.repeat(d_skip[l], SSM_HEAD_DIM)[None, :], ssm_norm_g=ssm_norm_g[l][None, :],
        w_a_out=w_a_out[l].astype(BF16), w_b_out=w_b_out[l].astype(BF16), w_c_out=w_c_out[l].astype(BF16),
        w_o=w_o[l].astype(BF16), norm2_g=norm2_g[l][None, :],
        w_up=w_up[l].astype(BF16), w_down=w_down[l].astype(BF16))


def _rope_tables(pos):
    inv = jnp.power(jnp.float32(ROPE_THETA), -jnp.arange(0, ATT_HEAD_DIM, 2, dtype=F32) / ATT_HEAD_DIM)
    ang = pos.astype(F32)[:, None] * inv[None, :]
    cos, sin = jnp.cos(ang), jnp.sin(ang)
    cos_h = jnp.concatenate([cos, cos], axis=1)
    sin_h = jnp.concatenate([-sin, sin], axis=1)
    return jnp.tile(cos_h, (1, ATT_HEADS)), jnp.tile(sin_h, (1, ATT_HEADS))


def kernel(x_prompt, x_sample, cache_k, cache_v, state_ssm, state_conv, page_table, norm1_g, w_in, b_gate, ln_v_g, ln_v_b, w_spatial, b_spatial, w_a_out, conv_w, conv_b, dt_bias, a_log, d_skip, ssm_norm_g, w_b_out, w_c_out, w_o, norm2_g, w_up, w_down, norm_f_g):
    bp, s_len, _ = x_prompt.shape
    db, t_len, _ = x_sample.shape
    depth = w_in.shape[0]
    past_len = page_table.shape[1] * PAGE_SIZE
    mp, ms = bp * s_len, db * t_len
    cos_p, sin_p = _rope_tables(jnp.arange(s_len))
    cos_s, sin_s = _rope_tables(past_len + jnp.arange(t_len))
    cos_s, sin_s = jnp.tile(cos_s, (db, 1)), jnp.tile(sin_s, (db, 1))
    conv_zero = jnp.zeros((1, bp, SSM_CONV - 1, CONV_DIM), F32)
    h_zero = jnp.zeros((1, bp, D_B, SSM_STATE), F32)
    gf = norm_f_g[None, :]
    tm_p = 1024
    cache_kt = jnp.transpose(cache_k, (0, 1, 3, 4, 2))
    cache_vt = jnp.transpose(cache_v, (0, 1, 3, 4, 2))

    xp = x_prompt.reshape(mp, D_MODEL)
    xs = x_sample.reshape(ms, D_MODEL)
    outs = {k: [] for k in ("a_v_s", "ssm_p", "ssm_s", "conv_p", "conv_s", "k_p", "v_p", "k_s", "v_s")}
    for l in range(depth):
        w = _prep_layer(l, norm1_g, w_in, b_gate, ln_v_g, ln_v_b, w_spatial, b_spatial, w_a_out, conv_w, conv_b,
                        dt_bias, a_log, d_skip, ssm_norm_g, w_b_out, w_c_out, w_o, norm2_g, w_up, w_down, t_len)
        last = l == depth - 1

        proj, v_att = _in_proj(xp, w["norm1_g"], w["w_in"], tm_p)
        ya = _branch_a_prompt(proj, w["ln_g"], w["ln_b"], w["w_tril"], w["bs_exp"])
        yb, h_new = _branch_b(proj, bp, SSM_CHUNK, conv_zero, h_zero, 0, w["conv_w"], w["conv_b"], w["dt_bias"],
                              w["a_log"], w["d_skip"], w["ssm_norm_g"])
        q_rot, k_rot, sel_bias = _rope_prompt(proj, bp, cos_p, sin_p)
        yc = _moba_prompt(q_rot, k_rot, v_att, sel_bias, bp)
        x1 = _merge(xp, ya, yb, yc, proj, w["b_gate"], w["w_a_out"], w["w_b_out"], w["w_c_out"], w["w_o"], 512)
        xp = _ffn(x1, w["norm2_g"], w["w_up"], w["w_down"], gf, tm_p, 1024, last)
        proj3 = proj.reshape(bp, s_len, N_COLS)
        outs["ssm_p"].append(h_new.reshape(bp, SSM_HEADS, SSM_HEAD_DIM, SSM_STATE))
        outs["conv_p"].append(proj3[:, s_len - (SSM_CONV - 1):, C_XBC:C_XBC + CONV_DIM])
        outs["k_p"].append(k_rot.reshape(bp, s_len, ATT_HEADS, ATT_HEAD_DIM))
        outs["v_p"].append(v_att.reshape(bp, s_len, ATT_HEADS, ATT_HEAD_DIM))

        proj, v_att = _in_proj(xs, w["norm1_g"], w["w_in"], ms)
        ya, vn = _branch_a_sample(proj, t_len, w["ln_g"], w["ln_b"], w["w_exp"], w["bs_exp_s"])
        yb, h_new = _branch_b(proj, db, t_len, state_conv, state_ssm.reshape(depth, db, D_B, SSM_STATE), l,
                              w["conv_w"], w["conv_b"], w["dt_bias"], w["a_log"], w["d_skip"], w["ssm_norm_g"])
        q_rot, k_rot = _rope_sample(proj, cos_s, sin_s)
        proj3 = proj.reshape(db, t_len, N_COLS)
        v_new = v_att.reshape(db, t_len, ATT_HEADS, ATT_HEAD_DIM)
        k_new = k_rot.reshape(db, t_len, ATT_HEADS, ATT_HEAD_DIM)
        q_htd = jnp.transpose(q_rot.reshape(db, t_len, ATT_HEADS, ATT_HEAD_DIM), (0, 2, 1, 3))
        o_htd = _moba_sample(q_htd, k_new, v_new, cache_kt, cache_vt, page_table, l)
        yc = jnp.transpose(o_htd, (0, 2, 1, 3)).reshape(ms, D_C).astype(BF16)
        x1 = _merge(xs, ya, yb, yc, proj, w["b_gate"], w["w_a_out"], w["w_b_out"], w["w_c_out"], w["w_o"], ms)
        xs = _ffn(x1, w["norm2_g"], w["w_up"], w["w_down"], gf, ms, 1024, last)
        outs["a_v_s"].append(vn.reshape(db, t_len, D_A))
        outs["ssm_s"].append(h_new.reshape(db, SSM_HEADS, SSM_HEAD_DIM, SSM_STATE))
        if t_len >= SSM_CONV - 1:
            conv_new = proj3[:, t_len - (SSM_CONV - 1):, C_XBC:C_XBC + CONV_DIM]
        else:
            conv_new = jnp.concatenate([state_conv[l], proj3[:, :, C_XBC:C_XBC + CONV_DIM]],
                                       axis=1)[:, -(SSM_CONV - 1):]
        outs["conv_s"].append(conv_new)
        outs["k_s"].append(k_new)
        outs["v_s"].append(v_new)

    st = lambda k: jnp.stack(outs[k])
    return (xp.reshape(bp, s_len, D_MODEL), xs.reshape(db, t_len, D_MODEL), st("a_v_s"), st("ssm_p"), st("ssm_s"),
            st("conv_p"), st("conv_s"), st("k_p"), st("v_p"), st("k_s"), st("v_s"))
```

```python
import functools
import math

import jax
import jax.numpy as jnp
from jax import lax
from jax.experimental import pallas as pl
from jax.experimental.pallas import tpu as pltpu

F32 = jnp.float32
BF16 = jnp.bfloat16
NEG_INF = float("-inf")

D_MODEL = 1024
PAGE_SIZE = 128
A_CHUNK = 128
D_A = D_MODEL // 2
A_GROUPS = 8
A_GROUP_DIM = D_A // A_GROUPS
D_B = D_MODEL
SSM_HEAD_DIM = 64
SSM_HEADS = D_B // SSM_HEAD_DIM
SSM_GROUPS = 4
SSM_STATE = 128
SSM_CONV = 4
SSM_CHUNK = 128
CONV_DIM = D_B + 2 * SSM_GROUPS * SSM_STATE
ATT_HEADS = 8
ATT_HEAD_DIM = 64
D_C = ATT_HEADS * ATT_HEAD_DIM
MOBA_BLOCK = 256
MOBA_TOPK = 3
ROPE_THETA = 10000.0
N_BRANCH = 3
D_FF = 4 * D_MODEL
EPS = 1e-6

LANES = 128
C_U, C_V, C_Z, C_XBC, C_GATE, C_Q, C_K, C_VATT, C_DT = 0, 512, 1024, 2048, 4096, 7168, 7680, 8192, 8704
DT_W = 256
N_COLS = C_DT + DT_W
IN_TN = 1280
VMEM_LIMIT = 56 * 1024 * 1024


def _cparams(sem):
    return pltpu.CompilerParams(dimension_semantics=sem, vmem_limit_bytes=VMEM_LIMIT)


def _dot(a, b):
    return jnp.dot(a, b, preferred_element_type=F32)


def _dot_nt(a, b):
    return lax.dot_general(a, b, (((1,), (1,)), ((), ())), preferred_element_type=F32)


def _split3(x):
    x1 = x.astype(BF16)
    r = x - x1.astype(F32)
    x2 = r.astype(BF16)
    r = r - x2.astype(F32)
    return x1, x2, r.astype(BF16)


def _dot_hi_nt(a, b):
    a1, a2, a3 = _split3(a)
    b1, b2, b3 = _split3(b)
    return (_dot_nt(a1, b1) + (_dot_nt(a1, b2) + _dot_nt(a2, b1))
            + (_dot_nt(a1, b3) + _dot_nt(a2, b2) + _dot_nt(a3, b1)))


def _rms(x, g):
    return x * lax.rsqrt(jnp.mean(x * x, axis=-1, keepdims=True) + EPS) * g


def _gelu(x):
    return 0.5 * x * (1.0 + lax.erf(x * math.sqrt(0.5)))


def _silu(x):
    return x * jax.nn.sigmoid(x)


def _inproj_kernel(x_ref, g_ref, w_ref, o_ref, v_ref, h_ref):
    j = pl.program_id(1)

    @pl.when(j == 0)
    def _():
        h_ref[...] = _rms(x_ref[...], g_ref[...]).astype(BF16)

    o_ref[...] = _dot(h_ref[...], w_ref[...])

    @pl.when(j == C_VATT // IN_TN)
    def _():
        v_ref[...] = o_ref[:, C_VATT % IN_TN:C_VATT % IN_TN + D_C]


def _in_proj(x2d, g, w_bf, tm):
    m = x2d.shape[0]
    assert C_VATT // IN_TN == (C_VATT + D_C - 1) // IN_TN
    return pl.pallas_call(
        _inproj_kernel,
        grid=(m // tm, N_COLS // IN_TN),
        in_specs=[pl.BlockSpec((tm, D_MODEL), lambda i, j: (i, 0)),
                  pl.BlockSpec((1, D_MODEL), lambda i, j: (0, 0)),
                  pl.BlockSpec((D_MODEL, IN_TN), lambda i, j: (0, j))],
        out_specs=[pl.BlockSpec((tm, IN_TN), lambda i, j: (i, j)),
                   pl.BlockSpec((tm, D_C), lambda i, j: (i, 0))],
        out_shape=[jax.ShapeDtypeStruct((m, N_COLS), F32), jax.ShapeDtypeStruct((m, D_C), F32)],
        scratch_shapes=[pltpu.VMEM((tm, D_MODEL), BF16)],
        compiler_params=_cparams(("parallel", "arbitrary")),
        name="in_proj",
    )(x2d, g, w_bf)


def _layernorm(x, g, b):
    xc = x - jnp.mean(x, axis=-1, keepdims=True)
    return xc * lax.rsqrt(jnp.mean(xc * xc, axis=-1, keepdims=True) + EPS) * g + b


A_ROWS = 4 * A_CHUNK


def _branch_a_kernel(u_ref, v_ref, g_ref, b_ref, w_ref, bs_ref, ya_ref):
    first = lax.broadcasted_iota(jnp.int32, (A_CHUNK, LANES), 1) < A_GROUP_DIM
    for c in range(A_ROWS // A_CHUNK):
        rows = slice(c * A_CHUNK, (c + 1) * A_CHUNK)
        gu = _gelu(u_ref[rows, :])
        vn = _layernorm(_gelu(v_ref[rows, :]), g_ref[...], b_ref[...]).astype(BF16)
        parts = []
        for p in range(A_GROUPS // 2):
            vp = vn[:, p * LANES:(p + 1) * LANES]
            parts.append(jnp.where(first, _dot(w_ref[2 * p], vp), _dot(w_ref[2 * p + 1], vp)))
        mixed = jnp.concatenate(parts, axis=1) + bs_ref[...]
        ya_ref[rows, :] = (gu * mixed).astype(BF16)


def _branch_a_prompt(proj, ln_g, ln_b, w_tril_bf, bs_exp):
    m = proj.shape[0]
    return pl.pallas_call(
        _branch_a_kernel,
        grid=(m // A_ROWS,),
        in_specs=[pl.BlockSpec((A_ROWS, D_A), lambda i: (i, C_U // D_A)),
                  pl.BlockSpec((A_ROWS, D_A), lambda i: (i, C_V // D_A)),
                  pl.BlockSpec((1, D_A), lambda i: (0, 0)),
                  pl.BlockSpec((1, D_A), lambda i: (0, 0)),
                  pl.BlockSpec((A_GROUPS, A_CHUNK, A_CHUNK), lambda i: (0, 0, 0)),
                  pl.BlockSpec((A_CHUNK, D_A), lambda i: (0, 0))],
        out_specs=pl.BlockSpec((A_ROWS, D_A), lambda i: (i, 0)),
        out_shape=jax.ShapeDtypeStruct((m, D_A), BF16),
        compiler_params=_cparams(("parallel",)),
        name="branch_a_prompt",
    )(proj, proj, ln_g, ln_b, w_tril_bf, bs_exp)


def _branch_a_sample_kernel(u_ref, v_ref, g_ref, b_ref, wexp_ref, bs_ref, ya_ref, vn_ref):
    t_len = bs_ref.shape[0]
    gu = _gelu(u_ref[...])
    vn = _layernorm(_gelu(v_ref[...]), g_ref[...], b_ref[...])
    vn_ref[...] = vn
    vn3 = vn.reshape(vn.shape[0] // t_len, t_len, D_A)
    mixed = jnp.broadcast_to(bs_ref[...][None], vn3.shape)
    for s in range(t_len):
        mixed = mixed + wexp_ref[s][None] * vn3[:, s:s + 1, :]
    ya_ref[...] = (gu * mixed.reshape(vn.shape)).astype(BF16)


def _branch_a_sample(proj, t_len, ln_g, ln_b, w_exp, bs_exp):
    m = proj.shape[0]
    return pl.pallas_call(
        _branch_a_sample_kernel,
        grid=(1,),
        in_specs=[pl.BlockSpec((m, D_A), lambda i: (0, C_U // D_A)),
                  pl.BlockSpec((m, D_A), lambda i: (0, C_V // D_A)),
                  pl.BlockSpec((1, D_A), lambda i: (0, 0)),
                  pl.BlockSpec((1, D_A), lambda i: (0, 0)),
                  pl.BlockSpec((t_len, t_len, D_A), lambda i: (0, 0, 0)),
                  pl.BlockSpec((t_len, D_A), lambda i: (0, 0))],
        out_specs=[pl.BlockSpec((m, D_A), lambda i: (0, 0)),
                   pl.BlockSpec((m, D_A), lambda i: (0, 0))],
        out_shape=[jax.ShapeDtypeStruct((m, D_A), BF16), jax.ShapeDtypeStruct((m, D_A), F32)],
        compiler_params=_cparams(("arbitrary",)),
        name="branch_a_sample",
    )(proj, proj, ln_g, ln_b, w_exp, bs_exp)


XP_OFF = 8


def _ssd_kernel(z_ref, xbc_ref, dt_ref, cprev_ref, h0_ref, cw_ref, cb_ref, dtb_ref, alog_ref, dsk_ref, ng_ref,
                yb_ref, hout_ref, xp_ref, *, rows):
    t_len = SSM_CHUNK
    c = pl.program_id(1)
    xbc = xbc_ref[...]

    def pad(a):
        if rows == t_len:
            return a
        return jnp.concatenate([a, jnp.zeros((t_len - rows, a.shape[1]), a.dtype)], axis=0)

    @pl.when(c == 0)
    def _():
        xp_ref[XP_OFF - 3:XP_OFF, :] = cprev_ref[...]
        hout_ref[...] = h0_ref[...]

    @pl.when(c > 0)
    def _():
        xp_ref[XP_OFF - 3:XP_OFF, :] = xp_ref[XP_OFF + rows - 3:XP_OFF + rows, :]

    xp_ref[XP_OFF:XP_OFF + rows, :] = xbc
    acc = cb_ref[...] + xbc * cw_ref[SSM_CONV - 1:SSM_CONV, :]
    for tap in range(SSM_CONV - 1):
        lo = XP_OFF - (SSM_CONV - 1) + tap
        acc = acc + xp_ref[lo:lo + rows, :] * cw_ref[tap:tap + 1, :]
    act = _silu(acc)
    xs_rows = act[:, :D_B]
    xs = pad(xs_rows)
    bm = pad(act[:, D_B:D_B + SSM_GROUPS * SSM_STATE]).astype(BF16)
    cm = pad(act[:, D_B + SSM_GROUPS * SSM_STATE:]).astype(BF16)

    dt = pad(jax.nn.softplus(dt_ref[...][:, :LANES] + dtb_ref[...]))
    row_i = lax.broadcasted_iota(jnp.int32, (t_len, t_len), 0)
    col_i = lax.broadcasted_iota(jnp.int32, (t_len, t_len), 1)
    a = dt * (-jnp.exp(alog_ref[...]))
    tril = jnp.where(col_i <= row_i, 1.0, 0.0).astype(BF16)
    a1, a2, a3 = _split3(a)
    acs = _dot(tril, a1) + _dot(tril, a2) + _dot(tril, a3)
    acs_t = acs.T
    dt_t = dt.T
    xs_t = xs.T
    causal_t = row_i <= col_i
    rep = SSM_HEADS // SSM_GROUPS
    ys = []
    for g in range(SSM_GROUPS):
        bg = bm[:, g * SSM_STATE:(g + 1) * SSM_STATE]
        cg = cm[:, g * SSM_STATE:(g + 1) * SSM_STATE]
        cb_t = _dot_nt(bg, cg)
        for r in range(rep):
            h = g * rep + r
            hs = slice(h * SSM_HEAD_DIM, (h + 1) * SSM_HEAD_DIM)
            acs_row = acs_t[h:h + 1, :]
            acs_col = acs[:, h:h + 1]
            dec_t = jnp.exp(jnp.where(causal_t, acs_row - acs_col, NEG_INF))
            w_t = (cb_t * dec_t).astype(BF16)
            xdt_t = xs_t[hs, :] * dt_t[h:h + 1, :]
            h_in = hout_ref[hs, :]
            y_t = _dot(xdt_t.astype(BF16), w_t) + _dot_nt(h_in.astype(BF16), cg) * jnp.exp(acs_row)
            last = acs_row[:, t_len - 1:t_len]
            st = _dot((xdt_t * jnp.exp(last - acs_row)).astype(BF16), bg)
            hout_ref[hs, :] = jnp.exp(last) * h_in + st
            ys.append(y_t)
    y = jnp.concatenate(ys, axis=0).T[:rows] + dsk_ref[...] * xs_rows
    yb_ref[...] = _rms(y * _silu(z_ref[...]), ng_ref[...]).astype(BF16)


def _branch_b(proj, n_seq, rows, conv_prev, h0, layer, conv_w, conv_b, dt_bias_p, a_log_p, d_skip_exp, norm_g):
    m = proj.shape[0]
    n_chunk = m // (n_seq * rows)
    const = lambda b, c: (0, 0)
    return pl.pallas_call(
        functools.partial(_ssd_kernel, rows=rows),
        grid=(n_seq, n_chunk),
        in_specs=[pl.BlockSpec((rows, D_B), lambda b, c: (b * n_chunk + c, C_Z // D_B)),
                  pl.BlockSpec((rows, CONV_DIM), lambda b, c: (b * n_chunk + c, C_XBC // CONV_DIM)),
                  pl.BlockSpec((rows, DT_W), lambda b, c: (b * n_chunk + c, C_DT // DT_W)),
                  pl.BlockSpec((None, None, SSM_CONV - 1, CONV_DIM), lambda b, c: (layer, b, 0, 0)),
                  pl.BlockSpec((None, None, D_B, SSM_STATE), lambda b, c: (layer, b, 0, 0)),
                  pl.BlockSpec((SSM_CONV, CONV_DIM), const),
                  pl.BlockSpec((1, CONV_DIM), const),
                  pl.BlockSpec((1, LANES), const),
                  pl.BlockSpec((1, LANES), const),
                  pl.BlockSpec((1, D_B), const),
                  pl.BlockSpec((1, D_B), const)],
        out_specs=[pl.BlockSpec((rows, D_B), lambda b, c: (b * n_chunk + c, 0)),
                   pl.BlockSpec((None, D_B, SSM_STATE), lambda b, c: (b, 0, 0))],
        out_shape=[jax.ShapeDtypeStruct((m, D_B), BF16),
                   jax.ShapeDtypeStruct((n_seq, D_B, SSM_STATE), F32)],
        scratch_shapes=[pltpu.VMEM((XP_OFF + SSM_CHUNK, CONV_DIM), F32)],
        compiler_params=_cparams(("parallel", "arbitrary")),
        name="branch_b_rows%d" % rows,
    )(proj, proj, proj, conv_prev, h0, conv_w, conv_b, dt_bias_p, a_log_p, d_skip_exp, norm_g)


def _rope_apply(x, cos, sin_signed, first):
    partner = jnp.where(first, pltpu.roll(x, D_C - ATT_HEAD_DIM // 2, 1), pltpu.roll(x, ATT_HEAD_DIM // 2, 1))
    return x * cos + partner * sin_signed


def _rope_kernel(q_ref, k_ref, cos_ref, sin_ref, qo_ref, ko_ref):
    lane = lax.broadcasted_iota(jnp.int32, q_ref.shape, 1)
    first = (lane % ATT_HEAD_DIM) < ATT_HEAD_DIM // 2
    cos, sin = cos_ref[...], sin_ref[...]
    qo_ref[...] = _rope_apply(q_ref[...], cos, sin, first)
    ko_ref[...] = _rope_apply(k_ref[...], cos, sin, first)


def _rope_gate_kernel(q_ref, k_ref, cos_ref, sin_ref, qo_ref, ko_ref, bias_ref, km_ref, *, nblk):
    step = pl.program_id(1)
    lane = lax.broadcasted_iota(jnp.int32, (MOBA_BLOCK, D_C), 1)
    first = (lane % ATT_HEAD_DIM) < ATT_HEAD_DIM // 2
    klane = lax.broadcasted_iota(jnp.int32, (nblk, D_C), 1) // ATT_HEAD_DIM
    bidx = lax.broadcasted_iota(jnp.int32, (nblk, MOBA_BLOCK), 0)
    n_sel = max(1, min(MOBA_TOPK, nblk - 1))

    @pl.when(step == 0)
    def _():
        km_ref[...] = jnp.zeros(km_ref.shape, F32)

    for r in range(ROPE_BLOCKS):
        j = step * ROPE_BLOCKS + r
        rows = slice(r * MOBA_BLOCK, (r + 1) * MOBA_BLOCK)
        cos, sin = cos_ref[rows, :], sin_ref[rows, :]
        qr = _rope_apply(q_ref[rows, :], cos, sin, first)
        kr = _rope_apply(k_ref[rows, :], cos, sin, first)
        qo_ref[rows, :] = qr
        ko_ref[rows, :] = kr
        km = km_ref[...]
        km_ref[pl.ds(j, 1), :] = jnp.mean(kr, axis=0, keepdims=True)
        kstack = jnp.concatenate([jnp.where(klane == h, km, 0.0) for h in range(ATT_HEADS)], axis=0)
        gate = _dot_hi_nt(kstack, qr)
        past = bidx < j
        for h in range(ATT_HEADS):
            g = jnp.where(past, gate[h * nblk:(h + 1) * nblk, :], NEG_INF)
            sel = past & (_topk_rank(g, bidx, nblk - 1, 0) < n_sel)
            bias_ref[h * nblk:(h + 1) * nblk, rows] = jnp.where(sel, 0.0, NEG_INF)


ROPE_BLOCKS = 4


def _rope_prompt(proj, n_seq, cos, sin):
    m = proj.shape[0]
    nblk = m // (n_seq * MOBA_BLOCK)
    n_step = nblk // ROPE_BLOCKS
    rows = ROPE_BLOCKS * MOBA_BLOCK
    row = lambda b, j: (b * n_step + j, 0)
    return pl.pallas_call(
        functools.partial(_rope_gate_kernel, nblk=nblk),
        grid=(n_seq, n_step),
        in_specs=[pl.BlockSpec((rows, D_C), lambda b, j: (b * n_step + j, C_Q // D_C)),
                  pl.BlockSpec((rows, D_C), lambda b, j: (b * n_step + j, C_K // D_C)),
                  pl.BlockSpec((rows, D_C), lambda b, j: (j, 0)),
                  pl.BlockSpec((rows, D_C), lambda b, j: (j, 0))],
        out_specs=[pl.BlockSpec((rows, D_C), row),
                   pl.BlockSpec((rows, D_C), row),
                   pl.BlockSpec((None, ATT_HEADS * nblk, rows), lambda b, j: (b, 0, j))],
        out_shape=[jax.ShapeDtypeStruct((m, D_C), F32), jax.ShapeDtypeStruct((m, D_C), F32),
                   jax.ShapeDtypeStruct((n_seq, ATT_HEADS * nblk, m // n_seq), F32)],
        scratch_shapes=[pltpu.VMEM((nblk, D_C), F32)],
        compiler_params=_cparams(("parallel", "arbitrary")),
        name="rope_prompt",
    )(proj, proj, cos, sin)


def _rope_sample(proj, cos, sin):
    m = proj.shape[0]
    return pl.pallas_call(
        _rope_kernel,
        grid=(1,),
        in_specs=[pl.BlockSpec((m, D_C), lambda i: (0, C_Q // D_C)),
                  pl.BlockSpec((m, D_C), lambda i: (0, C_K // D_C)),
                  pl.BlockSpec((m, D_C), lambda i: (0, 0)),
                  pl.BlockSpec((m, D_C), lambda i: (0, 0))],
        out_specs=[pl.BlockSpec((m, D_C), lambda i: (0, 0)), pl.BlockSpec((m, D_C), lambda i: (0, 0))],
        out_shape=[jax.ShapeDtypeStruct((m, D_C), F32), jax.ShapeDtypeStruct((m, D_C), F32)],
        compiler_params=_cparams(("arbitrary",)),
        name="rope_sample",
    )(proj, proj, cos, sin)


def _topk_rank(gate, idx, n_cand, axis):
    rank = jnp.zeros(gate.shape, jnp.int32)
    for i in range(n_cand):
        gi = gate[:, i:i + 1] if axis == 1 else gate[i:i + 1, :]
        beats = (gi > gate) | ((gi == gate) & (i < idx))
        rank = rank + jnp.where(beats, 1, 0)
    return rank


SUBLANES = 8
ONES_ROWS = 2 * SUBLANES


def _fold_rows(x, op):
    return op(x.reshape(x.shape[0] // SUBLANES, SUBLANES, x.shape[1]), axis=0)


def _moba_prompt_kernel(q_ref, k_ref, v_ref, bias_ref, o_ref, kb_ref, vt_ref, *, nblk):
    qb = pl.program_id(2)
    n_head = LANES // ATT_HEAD_DIM

    @pl.when(qb == 0)
    def _():
        ones = jnp.ones((ONES_ROWS, MOBA_BLOCK), BF16)
        for j in range(nblk):
            kb_ref[j] = k_ref[j * MOBA_BLOCK:(j + 1) * MOBA_BLOCK, :].astype(BF16)
            vt_ref[j] = jnp.concatenate([v_ref[j * MOBA_BLOCK:(j + 1) * MOBA_BLOCK, :].T.astype(BF16), ones], axis=0)

    q = q_ref[...] * (ATT_HEAD_DIM ** -0.5 * math.log2(math.e))
    lane = lax.broadcasted_iota(jnp.int32, (MOBA_BLOCK, LANES), 1)
    own_bias = jnp.where(lax.broadcasted_iota(jnp.int32, (MOBA_BLOCK, MOBA_BLOCK), 0)
                         <= lax.broadcasted_iota(jnp.int32, (MOBA_BLOCK, MOBA_BLOCK), 1), 0.0, NEG_INF)
    row = lax.broadcasted_iota(jnp.int32, (LANES, MOBA_BLOCK), 0)
    qms = [jnp.where((lane >= h * ATT_HEAD_DIM) & (lane < (h + 1) * ATT_HEAD_DIM), q, 0.0).astype(BF16)
           for h in range(n_head)]

    def attend(n_past):
        n_all = n_past + 1

        def score_block(h, j):
            st = _dot_nt(kb_ref[j], qms[h])
            if j == n_past:
                st = st + own_bias
                return st, None, _fold_rows(st, jnp.max)
            bias = bias_ref[h * nblk + j:h * nblk + j + 1, :]
            return st, bias, _fold_rows(st, jnp.max) + bias

        def fold_max(blocks):
            mx = blocks[0][2]
            for blk in blocks[1:]:
                mx = jnp.maximum(mx, blk[2])
            return jnp.max(mx, axis=0, keepdims=True)

        def weigh(blk, m, j, acc):
            st, bias, _ = blk
            p = jnp.exp2(st - (m if bias is None else m - bias))
            pv = _dot(vt_ref[j], p.astype(BF16))
            return pv if acc is None else acc + pv

        accs = []
        for h in range(n_head):
            blocks = [score_block(h, j) for j in range(n_all)]
            m = fold_max(blocks)
            acc = None
            for j in range(n_all):
                acc = weigh(blocks[j], m, j, acc)
            accs.append(acc)
        outs = [acc[:LANES] / acc[LANES:LANES + 1] for acc in accs]
        o_ref[...] = jnp.where(row < ATT_HEAD_DIM, outs[0], outs[1]).T.astype(BF16)

    for n_past in range(nblk):
        pl.when(qb == n_past)(functools.partial(attend, n_past))


def _moba_prompt(q_rot, k_rot, v_att, bias, n_seq):
    m = q_rot.shape[0]
    s_len = m // n_seq
    nblk = s_len // MOBA_BLOCK
    n_pair = D_C // LANES
    n_head = LANES // ATT_HEAD_DIM
    return pl.pallas_call(
        functools.partial(_moba_prompt_kernel, nblk=nblk),
        grid=(n_seq, n_pair, nblk),
        in_specs=[pl.BlockSpec((MOBA_BLOCK, LANES), lambda b, hp, i: (b * nblk + i, hp)),
                  pl.BlockSpec((s_len, LANES), lambda b, hp, i: (b, hp)),
                  pl.BlockSpec((s_len, LANES), lambda b, hp, i: (b, hp)),
                  pl.BlockSpec((None, n_head * nblk, MOBA_BLOCK), lambda b, hp, i: (b, hp, i))],
        out_specs=pl.BlockSpec((MOBA_BLOCK, LANES), lambda b, hp, i: (b * nblk + i, hp)),
        out_shape=jax.ShapeDtypeStruct((m, D_C), BF16),
        scratch_shapes=[pltpu.VMEM((nblk, MOBA_BLOCK, LANES), BF16),
                        pltpu.VMEM((nblk, LANES + ONES_ROWS, MOBA_BLOCK), BF16)],
        compiler_params=_cparams(("parallel", "parallel", "arbitrary")),
        name="moba_prompt",
    )(q_rot, k_rot, v_att, bias)


SAMPLE_PAGE_BUFFERS = 32
SAMPLE_PAGE_UNROLL = 8


def _moba_sample_kernel(pt_ref, q_ref, kn_ref, vn_ref, ck_hbm, cv_hbm, o_ref,
                        buf_ref, sem, s_ref, own_ref, oacc_ref, *, n_seq, n_pages, t_len, layer):
    nbuf = SAMPLE_PAGE_BUFFERS
    per_seq = 2 * n_pages
    total = n_seq * per_seq
    scale = ATT_HEAD_DIM ** -0.5
    ppb = MOBA_BLOCK // PAGE_SIZE
    n_blk = n_pages // ppb
    n_row = ATT_HEADS * t_len
    rows = lambda h: slice(h * t_len, (h + 1) * t_len)

    def page_copy(src_hbm, page, slot):
        return pltpu.make_async_copy(src_hbm.at[page, layer], buf_ref.at[slot], sem.at[slot])

    def start_fetch(n, slot):
        b = n // per_seq
        i = n % per_seq

        @pl.when(i < n_pages)
        def _():
            page_copy(ck_hbm, pt_ref[b, i], slot).start()

        @pl.when(i >= n_pages)
        def _():
            page_copy(cv_hbm, pt_ref[b, i - n_pages], slot).start()

    def finish_fetch(slot):
        page_copy(ck_hbm, 0, slot).wait()

    def refill(n, slot):
        @pl.when(n + nbuf < total)
        def _():
            start_fetch(n + nbuf, slot)

    b = pl.program_id(0)

    @pl.when(b == 0)
    def _():
        for n in range(nbuf):
            start_fetch(n, n)

    def one_sequence():
        base = b * per_seq

        grp = SAMPLE_PAGE_UNROLL

        def score_pages(g, c):
            pages = [g * grp + u for u in range(grp)]
            for i in pages:
                finish_fetch(i % nbuf)
            for i in pages:
                for h in range(ATT_HEADS):
                    sc = _dot(q_ref[h].astype(BF16), buf_ref[i % nbuf, h].astype(BF16))
                    s_ref[i, rows(h), :] = sc * scale
            for i in pages:
                refill(base + i, i % nbuf)
            return c

        lax.fori_loop(0, n_pages // grp, score_pages, 0)
        select_and_normalise(q_ref, kn_ref, vn_ref)

        def weigh_pages(g, accs):
            pages = [g * grp + u for u in range(grp)]
            for i in pages:
                finish_fetch((n_pages + i) % nbuf)
            accs = tuple(
                accs[h] + _dot_nt(
                    jnp.concatenate([s_ref[i, rows(h), :] for i in pages], axis=1).astype(BF16),
                    jnp.concatenate([buf_ref[(n_pages + i) % nbuf, h] for i in pages], axis=1).astype(BF16))
                for h in range(ATT_HEADS))
            for i in pages:
                refill(base + n_pages + i, (n_pages + i) % nbuf)
            return accs

        accs = lax.fori_loop(0, n_pages // grp, weigh_pages, tuple(oacc_ref[h] for h in range(ATT_HEADS)))
        for h in range(ATT_HEADS):
            o_ref[h] = accs[h] * own_ref[rows(h), 0:1]

    def select_and_normalise(q_ref, kn_ref, vn_ref):
        lane = lax.broadcasted_iota(jnp.int32, (n_row, LANES), 1)
        gate = jnp.full((n_row, LANES), NEG_INF, F32)
        for j in range(n_blk):
            tot = s_ref[ppb * j]
            for i in range(1, ppb):
                tot = tot + s_ref[ppb * j + i]
            gate = jnp.where(lane == j, jnp.sum(tot, axis=1, keepdims=True), gate)
        sel = jnp.where(_topk_rank(gate, lane, n_blk, 1) < min(MOBA_TOPK, n_blk), 1, 0)
        zpad = jnp.zeros((PAGE_SIZE - t_len, ATT_HEAD_DIM), F32)
        for h in range(ATT_HEADS):
            kh = jnp.concatenate([kn_ref[:, h, :], zpad], axis=0).astype(BF16)
            own_ref[rows(h), :] = _dot_nt(q_ref[h].astype(BF16), kh) * scale
        tok = lax.broadcasted_iota(jnp.int32, (n_row, LANES), 0) % t_len
        so = jnp.where(lane <= tok, own_ref[...], NEG_INF)
        mv = so
        for j in range(n_blk):
            for i in range(ppb):
                mv = jnp.maximum(mv, jnp.where(sel[:, j:j + 1] > 0, s_ref[ppb * j + i], NEG_INF))
        m = jnp.max(mv, axis=1, keepdims=True)
        po = jnp.exp(so - m)
        own_ref[...] = po
        lv = po
        for j in range(n_blk):
            for i in range(ppb):
                pj = jnp.exp(jnp.where(sel[:, j:j + 1] > 0, s_ref[ppb * j + i], NEG_INF) - m)
                s_ref[ppb * j + i] = pj
                lv = lv + pj
        inv = 1.0 / jnp.sum(lv, axis=1, keepdims=True)
        for h in range(ATT_HEADS):
            vh = jnp.concatenate([vn_ref[:, h, :], zpad], axis=0).astype(BF16)
            oacc_ref[h] = _dot(own_ref[rows(h), :].astype(BF16), vh)
        own_ref[:, 0:1] = inv

    one_sequence()


def _moba_sample(q_htd, k_thd, v_thd, cache_kt, cache_vt, page_table, layer):
    n_seq, _, t_len, _ = q_htd.shape
    n_pages = page_table.shape[1]
    assert (2 * n_pages) % SAMPLE_PAGE_BUFFERS == 0 and n_pages % SAMPLE_PAGE_UNROLL == 0
    tok_shape = (n_seq, ATT_HEADS, t_len, ATT_HEAD_DIM)
    per_seq = lambda shape: pl.BlockSpec((None,) + shape, lambda b, pt: (b, 0, 0, 0))
    tok_blk = per_seq((ATT_HEADS, t_len, ATT_HEAD_DIM))
    new_blk = per_seq((t_len, ATT_HEADS, ATT_HEAD_DIM))
    grid_spec = pltpu.PrefetchScalarGridSpec(
        num_scalar_prefetch=1,
        grid=(n_seq,),
        in_specs=[tok_blk, new_blk, new_blk, pl.BlockSpec(memory_space=pl.ANY), pl.BlockSpec(memory_space=pl.ANY)],
        out_specs=tok_blk,
        scratch_shapes=[pltpu.VMEM((SAMPLE_PAGE_BUFFERS, ATT_HEADS, ATT_HEAD_DIM, PAGE_SIZE), F32),
                        pltpu.SemaphoreType.DMA((SAMPLE_PAGE_BUFFERS,)),
                        pltpu.VMEM((n_pages, ATT_HEADS * t_len, PAGE_SIZE), F32),
                        pltpu.VMEM((ATT_HEADS * t_len, PAGE_SIZE), F32),
                        pltpu.VMEM((ATT_HEADS, t_len, ATT_HEAD_DIM), F32)])
    return pl.pallas_call(
        functools.partial(_moba_sample_kernel, n_seq=n_seq, n_pages=n_pages, t_len=t_len, layer=layer),
        grid_spec=grid_spec,
        out_shape=jax.ShapeDtypeStruct(tok_shape, F32),
        compiler_params=_cparams(("arbitrary",)),
        name="moba_sample",
    )(page_table, q_htd, k_thd, v_thd, cache_kt, cache_vt)


def _merge_kernel(x_ref, ya_ref, yb_ref, yc_ref, g0_ref, g1_ref, g2_ref, bg_ref, wa_ref, wb_ref, wc_ref, wo_ref, o_ref):
    bg = bg_ref[...]
    merged = (jax.nn.sigmoid(g0_ref[...] + bg[:, :D_MODEL]) * _dot(ya_ref[...], wa_ref[...])
              + jax.nn.sigmoid(g1_ref[...] + bg[:, D_MODEL:2 * D_MODEL]) * _dot(yb_ref[...], wb_ref[...])
              + jax.nn.sigmoid(g2_ref[...] + bg[:, 2 * D_MODEL:]) * _dot(yc_ref[...], wc_ref[...]))
    o_ref[...] = x_ref[...] + _dot(merged.astype(BF16), wo_ref[...])


def _merge(x2d, ya, yb, yc, proj, b_gate, wa, wb, wc, wo, tm):
    m = x2d.shape[0]
    const = lambda i: (0, 0)
    gate_spec = lambda k: pl.BlockSpec((tm, D_MODEL), lambda i: (i, C_GATE // D_MODEL + k))
    return pl.pallas_call(
        _merge_kernel,
        grid=(m // tm,),
        in_specs=[pl.BlockSpec((tm, D_MODEL), lambda i: (i, 0)),
                  pl.BlockSpec((tm, D_A), lambda i: (i, 0)),
                  pl.BlockSpec((tm, D_B), lambda i: (i, 0)),
                  pl.BlockSpec((tm, D_C), lambda i: (i, 0)),
                  gate_spec(0), gate_spec(1), gate_spec(2),
                  pl.BlockSpec((1, N_BRANCH * D_MODEL), const),
                  pl.BlockSpec((D_A, D_MODEL), const),
                  pl.BlockSpec((D_B, D_MODEL), const),
                  pl.BlockSpec((D_C, D_MODEL), const),
                  pl.BlockSpec((D_MODEL, D_MODEL), const)],
        out_specs=pl.BlockSpec((tm, D_MODEL), lambda i: (i, 0)),
        out_shape=jax.ShapeDtypeStruct((m, D_MODEL), F32),
        compiler_params=_cparams(("parallel",)),
        name="merge",
    )(x2d, ya, yb, yc, proj, proj, proj, b_gate, wa, wb, wc, wo)


def _ffn_kernel(x_ref, g_ref, wu_ref, wd_ref, gf_ref, o_ref, h_ref, acc_ref, *, final_norm):
    j = pl.program_id(1)

    @pl.when(j == 0)
    def _():
        h_ref[...] = _rms(x_ref[...], g_ref[...]).astype(BF16)
        acc_ref[...] = x_ref[...]

    f = jnp.maximum(_dot(h_ref[...], wu_ref[...]), 0.0)
    acc_ref[...] += _dot((f * f).astype(BF16), wd_ref[...])

    @pl.when(j == pl.num_programs(1) - 1)
    def _():
        y = acc_ref[...]
        o_ref[...] = _rms(y, gf_ref[...]) if final_norm else y


def _ffn(x2d, g, wu, wd, gf, tm, tf, final_norm):
    m = x2d.shape[0]
    return pl.pallas_call(
        functools.partial(_ffn_kernel, final_norm=final_norm),
        grid=(m // tm, D_FF // tf),
        in_specs=[pl.BlockSpec((tm, D_MODEL), lambda i, j: (i, 0)),
                  pl.BlockSpec((1, D_MODEL), lambda i, j: (0, 0)),
                  pl.BlockSpec((D_MODEL, tf), lambda i, j: (0, j)),
                  pl.BlockSpec((tf, D_MODEL), lambda i, j: (j, 0)),
                  pl.BlockSpec((1, D_MODEL), lambda i, j: (0, 0))],
        out_specs=pl.BlockSpec((tm, D_MODEL), lambda i, j: (i, 0)),
        out_shape=jax.ShapeDtypeStruct((m, D_MODEL), F32),
        scratch_shapes=[pltpu.VMEM((tm, D_MODEL), BF16), pltpu.VMEM((tm, D_MODEL), F32)],
        compiler_params=_cparams(("parallel", "arbitrary")),
        name="ffn",
    )(x2d, g, wu, wd, gf)


def _prep_layer(l, norm1_g, w_in, b_gate, ln_v_g, ln_v_b, w_spatial, b_spatial, w_a_out, conv_w, conv_b,
                dt_bias, a_log, d_skip, ssm_norm_g, w_b_out, w_c_out, w_o, norm2_g, w_up, w_down, t_len):
    w = w_in[l]
    o_dt = 2 * D_A + D_B + CONV_DIM
    o_q = o_dt + SSM_HEADS
    o_gate = o_q + 3 * D_C
    w_perm = jnp.concatenate(
        [w[:, :o_dt], w[:, o_gate:], w[:, o_q:o_gate], w[:, o_dt:o_q],
         jnp.zeros((D_MODEL, DT_W - SSM_HEADS), w.dtype)], axis=1).astype(BF16)
    tril = jnp.tril(jnp.ones((A_CHUNK, A_CHUNK), bool))
    w_tril = jnp.where(tril[None], w_spatial[l], 0)
    bs_exp = jnp.repeat(b_spatial[l].T, A_GROUP_DIM, axis=1)
    w_exp = jnp.repeat(jnp.transpose(w_tril[:, :t_len, :t_len], (2, 1, 0)), A_GROUP_DIM, axis=2)
    pad_h = lambda v: jnp.concatenate([v, jnp.zeros((LANES - SSM_HEADS,), v.dtype)])[None, :]
    return dict(
        norm1_g=norm1_g[l][None, :], w_in=w_perm, b_gate=b_gate[l][None, :],
        ln_g=ln_v_g[l][None, :], ln_b=ln_v_b[l][None, :],
        w_tril=w_tril.astype(BF16), bs_exp=bs_exp, w_exp=w_exp, bs_exp_s=bs_exp[:t_len],
        conv_w=conv_w[l], conv_b=conv_b[l][None, :], dt_bias=pad_h(dt_bias[l]), a_log=pad_h(a_log[l]),
        d_skip=jnp.repeat(d_skip[l], SSM_HEAD_DIM)[None, :], ssm_norm_g=ssm_norm_g[l][None, :],
        w_a_out=w_a_out[l].astype(BF16), w_b_out=w_b_out[l].astype(BF16), w_c_out=w_c_out[l].astype(BF16),
        w_o=w_o[l].astype(BF16), norm2_g=norm2_g[l][None, :],
        w_up=w_up[l].astype(BF16), w_down=w_down[l].astype(BF16))


def _rope_tables(pos):
    inv = jnp.power(jnp.float32(ROPE_THETA), -jnp.arange(0, ATT_HEAD_DIM, 2, dtype=F32) / ATT_HEAD_DIM)
    ang = pos.astype(F32)[:, None] * inv[None, :]
    cos, sin = jnp.cos(ang), jnp.sin(ang)
    cos_h = jnp.concatenate([cos, cos], axis=1)
    sin_h = jnp.concatenate([-sin, sin], axis=1)
    return jnp.tile(cos_h, (1, ATT_HEADS)), jnp.tile(sin_h, (1, ATT_HEADS))


def kernel(x_prompt, x_sample, cache_k, cache_v, state_ssm, state_conv, page_table, norm1_g, w_in, b_gate, ln_v_g, ln_v_b, w_spatial, b_spatial, w_a_out, conv_w, conv_b, dt_bias, a_log, d_skip, ssm_norm_g, w_b_out, w_c_out, w_o, norm2_g, w_up, w_down, norm_f_g):
    bp, s_len, _ = x_prompt.shape
    db, t_len, _ = x_sample.shape
    depth = w_in.shape[0]
    past_len = page_table.shape[1] * PAGE_SIZE
    mp, ms = bp * s_len, db * t_len
    cos_p, sin_p = _rope_tables(jnp.arange(s_len))
    cos_s, sin_s = _rope_tables(past_len + jnp.arange(t_len))
    cos_s, sin_s = jnp.tile(cos_s, (db, 1)), jnp.tile(sin_s, (db, 1))
    conv_zero = jnp.zeros((1, bp, SSM_CONV - 1, CONV_DIM), F32)
    h_zero = jnp.zeros((1, bp, D_B, SSM_STATE), F32)
    gf = norm_f_g[None, :]
    tm_p = 1024
    cache_kt = jnp.transpose(cache_k, (0, 1, 3, 4, 2))
    cache_vt = jnp.transpose(cache_v, (0, 1, 3, 4, 2))

    xp = x_prompt.reshape(mp, D_MODEL)
    xs = x_sample.reshape(ms, D_MODEL)
    outs = {k: [] for k in ("a_v_s", "ssm_p", "ssm_s", "conv_p", "conv_s", "k_p", "v_p", "k_s", "v_s")}
    for l in range(depth):
        w = _prep_layer(l, norm1_g, w_in, b_gate, ln_v_g, ln_v_b, w_spatial, b_spatial, w_a_out, conv_w, conv_b,
                        dt_bias, a_log, d_skip, ssm_norm_g, w_b_out, w_c_out, w_o, norm2_g, w_up, w_down, t_len)
        last = l == depth - 1

        proj, v_att = _in_proj(xp, w["norm1_g"], w["w_in"], tm_p)
        ya = _branch_a_prompt(proj, w["ln_g"], w["ln_b"], w["w_tril"], w["bs_exp"])
        yb, h_new = _branch_b(proj, bp, SSM_CHUNK, conv_zero, h_zero, 0, w["conv_w"], w["conv_b"], w["dt_bias"],
                              w["a_log"], w["d_skip"], w["ssm_norm_g"])
        q_rot, k_rot, sel_bias = _rope_prompt(proj, bp, cos_p, sin_p)
        yc = _moba_prompt(q_rot, k_rot, v_att, sel_bias, bp)
        x1 = _merge(xp, ya, yb, yc, proj, w["b_gate"], w["w_a_out"], w["w_b_out"], w["w_c_out"], w["w_o"], 512)
        xp = _ffn(x1, w["norm2_g"], w["w_up"], w["w_down"], gf, tm_p, 1024, last)
        proj3 = proj.reshape(bp, s_len, N_COLS)
        outs["ssm_p"].append(h_new.reshape(bp, SSM_HEADS, SSM_HEAD_DIM, SSM_STATE))
        outs["conv_p"].append(proj3[:, s_len - (SSM_CONV - 1):, C_XBC:C_XBC + CONV_DIM])
        outs["k_p"].append(k_rot.reshape(bp, s_len, ATT_HEADS, ATT_HEAD_DIM))
        outs["v_p"].append(v_att.reshape(bp, s_len, ATT_HEADS, ATT_HEAD_DIM))

        proj, v_att = _in_proj(xs, w["norm1_g"], w["w_in"], ms)
        ya, vn = _branch_a_sample(proj, t_len, w["ln_g"], w["ln_b"], w["w_exp"], w["bs_exp_s"])
        yb, h_new = _branch_b(proj, db, t_len, state_conv, state_ssm.reshape(depth, db, D_B, SSM_STATE), l,
                              w["conv_w"], w["conv_b"], w["dt_bias"], w["a_log"], w["d_skip"], w["ssm_norm_g"])
        q_rot, k_rot = _rope_sample(proj, cos_s, sin_s)
        proj3 = proj.reshape(db, t_len, N_COLS)
        v_new = v_att.reshape(db, t_len, ATT_HEADS, ATT_HEAD_DIM)
        k_new = k_rot.reshape(db, t_len, ATT_HEADS, ATT_HEAD_DIM)
        q_htd = jnp.transpose(q_rot.reshape(db, t_len, ATT_HEADS, ATT_HEAD_DIM), (0, 2, 1, 3))
        o_htd = _moba_sample(q_htd, k_new, v_new, cache_kt, cache_vt, page_table, l)
        yc = jnp.transpose(o_htd, (0, 2, 1, 3)).reshape(ms, D_C).astype(BF16)
        x1 = _merge(xs, ya, yb, yc, proj, w["b_gate"], w["w_a_out"], w["w_b_out"], w["w_c_out"], w["w_o"], ms)
        xs = _ffn(x1, w["norm2_g"], w["w_up"], w["w_down"], gf, ms, 1024, last)
        outs["a_v_s"].append(vn.reshape(db, t_len, D_A))
        outs["ssm_s"].append(h_new.reshape(db, SSM_HEADS, SSM_HEAD_DIM, SSM_STATE))
        if t_len >= SSM_CONV - 1:
            conv_new = proj3[:, t_len - (SSM_CONV - 1):, C_XBC:C_XBC + CONV_DIM]
        else:
            conv_new = jnp.concatenate([state_conv[l], proj3[:, :, C_XBC:C_XBC + CONV_DIM]],
                                       axis=1)[:, -(SSM_CONV - 1):]
        outs["conv_s"].append(conv_new)
        outs["k_s"].append(k_new)
        outs["v_s"].append(v_new)

    st = lambda k: jnp.stack(outs[k])
    return (xp.reshape(bp, s_len, D_MODEL), xs.reshape(db, t_len, D_MODEL), st("a_v_s"), st("ssm_p"), st("ssm_s"),
            st("conv_p"), st("conv_s"), st("k_p"), st("v_p"), st("k_s"), st("v_s"))
```

```python
import functools
import math

import jax
import jax.numpy as jnp
from jax import lax
from jax.experimental import pallas as pl
from jax.experimental.pallas import tpu as pltpu

F32 = jnp.float32
BF16 = jnp.bfloat16
NEG_INF = float("-inf")

D_MODEL = 1024
PAGE_SIZE = 128
A_CHUNK = 128
D_A = D_MODEL // 2
A_GROUPS = 8
A_GROUP_DIM = D_A // A_GROUPS
D_B = D_MODEL
SSM_HEAD_DIM = 64
SSM_HEADS = D_B // SSM_HEAD_DIM
SSM_GROUPS = 4
SSM_STATE = 128
SSM_CONV = 4
SSM_CHUNK = 128
CONV_DIM = D_B + 2 * SSM_GROUPS * SSM_STATE
ATT_HEADS = 8
ATT_HEAD_DIM = 64
D_C = ATT_HEADS * ATT_HEAD_DIM
MOBA_BLOCK = 256
MOBA_TOPK = 3
ROPE_THETA = 10000.0
N_BRANCH = 3
D_FF = 4 * D_MODEL
EPS = 1e-6

LANES = 128
C_U, C_V, C_Z, C_XBC, C_GATE, C_Q, C_K, C_VATT, C_DT = 0, 512, 1024, 2048, 4096, 7168, 7680, 8192, 8704
DT_W = 256
N_COLS = C_DT + DT_W
IN_TN = 1280
VMEM_LIMIT = 56 * 1024 * 1024


def _cparams(sem):
    return pltpu.CompilerParams(dimension_semantics=sem, vmem_limit_bytes=VMEM_LIMIT)


def _dot(a, b):
    return jnp.dot(a, b, preferred_element_type=F32)


def _dot_nt(a, b):
    return lax.dot_general(a, b, (((1,), (1,)), ((), ())), preferred_element_type=F32)


def _split3(x):
    x1 = x.astype(BF16)
    r = x - x1.astype(F32)
    x2 = r.astype(BF16)
    r = r - x2.astype(F32)
    return x1, x2, r.astype(BF16)


def _dot_hi_nt(a, b):
    a1, a2, a3 = _split3(a)
    b1, b2, b3 = _split3(b)
    return (_dot_nt(a1, b1) + (_dot_nt(a1, b2) + _dot_nt(a2, b1))
            + (_dot_nt(a1, b3) + _dot_nt(a2, b2) + _dot_nt(a3, b1)))


def _rms(x, g):
    return x * lax.rsqrt(jnp.mean(x * x, axis=-1, keepdims=True) + EPS) * g


def _gelu(x):
    return 0.5 * x * (1.0 + lax.erf(x * math.sqrt(0.5)))


def _silu(x):
    return x * jax.nn.sigmoid(x)


def _inproj_kernel(x_ref, g_ref, w_ref, o_ref, v_ref, h_ref):
    j = pl.program_id(1)

    @pl.when(j == 0)
    def _():
        h_ref[...] = _rms(x_ref[...], g_ref[...]).astype(BF16)

    o_ref[...] = _dot(h_ref[...], w_ref[...])

    @pl.when(j == C_VATT // IN_TN)
    def _():
        v_ref[...] = o_ref[:, C_VATT % IN_TN:C_VATT % IN_TN + D_C]


def _in_proj(x2d, g, w_bf, tm):
    m = x2d.shape[0]
    assert C_VATT // IN_TN == (C_VATT + D_C - 1) // IN_TN
    return pl.pallas_call(
        _inproj_kernel,
        grid=(m // tm, N_COLS // IN_TN),
        in_specs=[pl.BlockSpec((tm, D_MODEL), lambda i, j: (i, 0)),
                  pl.BlockSpec((1, D_MODEL), lambda i, j: (0, 0)),
                  pl.BlockSpec((D_MODEL, IN_TN), lambda i, j: (0, j))],
        out_specs=[pl.BlockSpec((tm, IN_TN), lambda i, j: (i, j)),
                   pl.BlockSpec((tm, D_C), lambda i, j: (i, 0))],
        out_shape=[jax.ShapeDtypeStruct((m, N_COLS), F32), jax.ShapeDtypeStruct((m, D_C), F32)],
        scratch_shapes=[pltpu.VMEM((tm, D_MODEL), BF16)],
        compiler_params=_cparams(("parallel", "arbitrary")),
        name="in_proj",
    )(x2d, g, w_bf)


def _layernorm(x, g, b):
    xc = x - jnp.mean(x, axis=-1, keepdims=True)
    return xc * lax.rsqrt(jnp.mean(xc * xc, axis=-1, keepdims=True) + EPS) * g + b


A_ROWS = 4 * A_CHUNK


def _branch_a_kernel(u_ref, v_ref, g_ref, b_ref, w_ref, bs_ref, ya_ref):
    first = lax.broadcasted_iota(jnp.int32, (A_CHUNK, LANES), 1) < A_GROUP_DIM
    for c in range(A_ROWS // A_CHUNK):
        rows = slice(c * A_CHUNK, (c + 1) * A_CHUNK)
        gu = _gelu(u_ref[rows, :])
        vn = _layernorm(_gelu(v_ref[rows, :]), g_ref[...], b_ref[...]).astype(BF16)
        parts = []
        for p in range(A_GROUPS // 2):
            vp = vn[:, p * LANES:(p + 1) * LANES]
            parts.append(jnp.where(first, _dot(w_ref[2 * p], vp), _dot(w_ref[2 * p + 1], vp)))
        mixed = jnp.concatenate(parts, axis=1) + bs_ref[...]
        ya_ref[rows, :] = (gu * mixed).astype(BF16)


def _branch_a_prompt(proj, ln_g, ln_b, w_tril_bf, bs_exp):
    m = proj.shape[0]
    return pl.pallas_call(
        _branch_a_kernel,
        grid=(m // A_ROWS,),
        in_specs=[pl.BlockSpec((A_ROWS, D_A), lambda i: (i, C_U // D_A)),
                  pl.BlockSpec((A_ROWS, D_A), lambda i: (i, C_V // D_A)),
                  pl.BlockSpec((1, D_A), lambda i: (0, 0)),
                  pl.BlockSpec((1, D_A), lambda i: (0, 0)),
                  pl.BlockSpec((A_GROUPS, A_CHUNK, A_CHUNK), lambda i: (0, 0, 0)),
                  pl.BlockSpec((A_CHUNK, D_A), lambda i: (0, 0))],
        out_specs=pl.BlockSpec((A_ROWS, D_A), lambda i: (i, 0)),
        out_shape=jax.ShapeDtypeStruct((m, D_A), BF16),
        compiler_params=_cparams(("parallel",)),
        name="branch_a_prompt",
    )(proj, proj, ln_g, ln_b, w_tril_bf, bs_exp)


def _branch_a_sample_kernel(u_ref, v_ref, g_ref, b_ref, wexp_ref, bs_ref, ya_ref, vn_ref):
    t_len = bs_ref.shape[0]
    gu = _gelu(u_ref[...])
    vn = _layernorm(_gelu(v_ref[...]), g_ref[...], b_ref[...])
    vn_ref[...] = vn
    vn3 = vn.reshape(vn.shape[0] // t_len, t_len, D_A)
    mixed = jnp.broadcast_to(bs_ref[...][None], vn3.shape)
    for s in range(t_len):
        mixed = mixed + wexp_ref[s][None] * vn3[:, s:s + 1, :]
    ya_ref[...] = (gu * mixed.reshape(vn.shape)).astype(BF16)


def _branch_a_sample(proj, t_len, ln_g, ln_b, w_exp, bs_exp):
    m = proj.shape[0]
    return pl.pallas_call(
        _branch_a_sample_kernel,
        grid=(1,),
        in_specs=[pl.BlockSpec((m, D_A), lambda i: (0, C_U // D_A)),
                  pl.BlockSpec((m, D_A), lambda i: (0, C_V // D_A)),
                  pl.BlockSpec((1, D_A), lambda i: (0, 0)),
                  pl.BlockSpec((1, D_A), lambda i: (0, 0)),
                  pl.BlockSpec((t_len, t_len, D_A), lambda i: (0, 0, 0)),
                  pl.BlockSpec((t_len, D_A), lambda i: (0, 0))],
        out_specs=[pl.BlockSpec((m, D_A), lambda i: (0, 0)),
                   pl.BlockSpec((m, D_A), lambda i: (0, 0))],
        out_shape=[jax.ShapeDtypeStruct((m, D_A), BF16), jax.ShapeDtypeStruct((m, D_A), F32)],
        compiler_params=_cparams(("arbitrary",)),
        name="branch_a_sample",
    )(proj, proj, ln_g, ln_b, w_exp, bs_exp)


XP_OFF = 8
SSD_CHUNKS_PER_STEP = 2


def _ssd_kernel(*refs, rows, chunks):
    for cc in range(chunks):
        _ssd_chunk(pl.program_id(1) * chunks + cc, slice(cc * rows, (cc + 1) * rows), *refs, rows=rows)


def _ssd_chunk(c, rs, z_ref, xbc_ref, dt_ref, cprev_ref, h0_ref, cw_ref, cb_ref, dtb_ref, alog_ref, dsk_ref, ng_ref,
               yb_ref, hout_ref, xp_ref, *, rows):
    t_len = SSM_CHUNK
    xbc = xbc_ref[rs, :]

    def pad(a):
        if rows == t_len:
            return a
        return jnp.concatenate([a, jnp.zeros((t_len - rows, a.shape[1]), a.dtype)], axis=0)

    @pl.when(c == 0)
    def _():
        xp_ref[XP_OFF - 3:XP_OFF, :] = cprev_ref[...]
        hout_ref[...] = h0_ref[...]

    @pl.when(c > 0)
    def _():
        xp_ref[XP_OFF - 3:XP_OFF, :] = xp_ref[XP_OFF + rows - 3:XP_OFF + rows, :]

    xp_ref[XP_OFF:XP_OFF + rows, :] = xbc
    acc = cb_ref[...] + xbc * cw_ref[SSM_CONV - 1:SSM_CONV, :]
    for tap in range(SSM_CONV - 1):
        lo = XP_OFF - (SSM_CONV - 1) + tap
        acc = acc + xp_ref[lo:lo + rows, :] * cw_ref[tap:tap + 1, :]
    act = _silu(acc)
    xs_rows = act[:, :D_B]
    xs = pad(xs_rows)
    bm = pad(act[:, D_B:D_B + SSM_GROUPS * SSM_STATE]).astype(BF16)
    cm = pad(act[:, D_B + SSM_GROUPS * SSM_STATE:]).astype(BF16)

    dt = pad(jax.nn.softplus(dt_ref[rs, :][:, :LANES] + dtb_ref[...]))
    row_i = lax.broadcasted_iota(jnp.int32, (t_len, t_len), 0)
    col_i = lax.broadcasted_iota(jnp.int32, (t_len, t_len), 1)
    a = dt * (-jnp.exp(alog_ref[...]))
    tril = jnp.where(col_i <= row_i, 1.0, 0.0).astype(BF16)
    a1, a2, a3 = _split3(a)
    acs = _dot(tril, a1) + _dot(tril, a2) + _dot(tril, a3)
    acs_t = acs.T
    dt_t = dt.T
    xs_t = xs.T
    causal_t = row_i <= col_i
    rep = SSM_HEADS // SSM_GROUPS
    ys = []
    for g in range(SSM_GROUPS):
        bg = bm[:, g * SSM_STATE:(g + 1) * SSM_STATE]
        cg = cm[:, g * SSM_STATE:(g + 1) * SSM_STATE]
        cb_t = _dot_nt(bg, cg)
        for r in range(rep):
            h = g * rep + r
            hs = slice(h * SSM_HEAD_DIM, (h + 1) * SSM_HEAD_DIM)
            acs_row = acs_t[h:h + 1, :]
            acs_col = acs[:, h:h + 1]
            dec_t = jnp.exp(jnp.where(causal_t, acs_row - acs_col, NEG_INF))
            w_t = (cb_t * dec_t).astype(BF16)
            xdt_t = xs_t[hs, :] * dt_t[h:h + 1, :]
            h_in = hout_ref[hs, :]
            y_t = _dot(xdt_t.astype(BF16), w_t) + _dot_nt(h_in.astype(BF16), cg) * jnp.exp(acs_row)
            last = acs_row[:, t_len - 1:t_len]
            st = _dot((xdt_t * jnp.exp(last - acs_row)).astype(BF16), bg)
            hout_ref[hs, :] = jnp.exp(last) * h_in + st
            ys.append(y_t)
    y = jnp.concatenate(ys, axis=0).T[:rows] + dsk_ref[...] * xs_rows
    yb_ref[rs, :] = _rms(y * _silu(z_ref[rs, :]), ng_ref[...]).astype(BF16)


def _branch_b(proj, n_seq, rows, conv_prev, h0, layer, conv_w, conv_b, dt_bias_p, a_log_p, d_skip_exp, norm_g):
    m = proj.shape[0]
    n_chunk = m // (n_seq * rows)
    chunks = SSD_CHUNKS_PER_STEP if n_chunk % SSD_CHUNKS_PER_STEP == 0 else 1
    n_step = n_chunk // chunks
    blk = rows * chunks
    const = lambda b, c: (0, 0)
    return pl.pallas_call(
        functools.partial(_ssd_kernel, rows=rows, chunks=chunks),
        grid=(n_seq, n_step),
        in_specs=[pl.BlockSpec((blk, D_B), lambda b, c: (b * n_step + c, C_Z // D_B)),
                  pl.BlockSpec((blk, CONV_DIM), lambda b, c: (b * n_step + c, C_XBC // CONV_DIM)),
                  pl.BlockSpec((blk, DT_W), lambda b, c: (b * n_step + c, C_DT // DT_W)),
                  pl.BlockSpec((None, None, SSM_CONV - 1, CONV_DIM), lambda b, c: (layer, b, 0, 0)),
                  pl.BlockSpec((None, None, D_B, SSM_STATE), lambda b, c: (layer, b, 0, 0)),
                  pl.BlockSpec((SSM_CONV, CONV_DIM), const),
                  pl.BlockSpec((1, CONV_DIM), const),
                  pl.BlockSpec((1, LANES), const),
                  pl.BlockSpec((1, LANES), const),
                  pl.BlockSpec((1, D_B), const),
                  pl.BlockSpec((1, D_B), const)],
        out_specs=[pl.BlockSpec((blk, D_B), lambda b, c: (b * n_step + c, 0)),
                   pl.BlockSpec((None, D_B, SSM_STATE), lambda b, c: (b, 0, 0))],
        out_shape=[jax.ShapeDtypeStruct((m, D_B), BF16),
                   jax.ShapeDtypeStruct((n_seq, D_B, SSM_STATE), F32)],
        scratch_shapes=[pltpu.VMEM((XP_OFF + SSM_CHUNK, CONV_DIM), F32)],
        compiler_params=_cparams(("parallel", "arbitrary")),
        name="branch_b_rows%d" % rows,
    )(proj, proj, proj, conv_prev, h0, conv_w, conv_b, dt_bias_p, a_log_p, d_skip_exp, norm_g)


def _rope_apply(x, cos, sin_signed, first):
    partner = jnp.where(first, pltpu.roll(x, D_C - ATT_HEAD_DIM // 2, 1), pltpu.roll(x, ATT_HEAD_DIM // 2, 1))
    return x * cos + partner * sin_signed


def _rope_kernel(q_ref, k_ref, cos_ref, sin_ref, qo_ref, ko_ref):
    lane = lax.broadcasted_iota(jnp.int32, q_ref.shape, 1)
    first = (lane % ATT_HEAD_DIM) < ATT_HEAD_DIM // 2
    cos, sin = cos_ref[...], sin_ref[...]
    qo_ref[...] = _rope_apply(q_ref[...], cos, sin, first)
    ko_ref[...] = _rope_apply(k_ref[...], cos, sin, first)


def _rope_gate_kernel(q_ref, k_ref, cos_ref, sin_ref, qo_ref, ko_ref, bias_ref, km_ref, *, nblk):
    step = pl.program_id(1)
    lane = lax.broadcasted_iota(jnp.int32, (MOBA_BLOCK, D_C), 1)
    first = (lane % ATT_HEAD_DIM) < ATT_HEAD_DIM // 2
    klane = lax.broadcasted_iota(jnp.int32, (nblk, D_C), 1) // ATT_HEAD_DIM
    bidx = lax.broadcasted_iota(jnp.int32, (nblk, MOBA_BLOCK), 0)
    n_sel = max(1, min(MOBA_TOPK, nblk - 1))

    @pl.when(step == 0)
    def _():
        km_ref[...] = jnp.zeros(km_ref.shape, F32)

    for r in range(ROPE_BLOCKS):
        j = step * ROPE_BLOCKS + r
        rows = slice(r * MOBA_BLOCK, (r + 1) * MOBA_BLOCK)
        cos, sin = cos_ref[rows, :], sin_ref[rows, :]
        qr = _rope_apply(q_ref[rows, :], cos, sin, first)
        kr = _rope_apply(k_ref[rows, :], cos, sin, first)
        qo_ref[rows, :] = qr
        ko_ref[rows, :] = kr
        km = km_ref[...]
        km_ref[pl.ds(j, 1), :] = jnp.mean(kr, axis=0, keepdims=True)
        kstack = jnp.concatenate([jnp.where(klane == h, km, 0.0) for h in range(ATT_HEADS)], axis=0)
        gate = _dot_hi_nt(kstack, qr)
        past = bidx < j
        for h in range(ATT_HEADS):
            g = jnp.where(past, gate[h * nblk:(h + 1) * nblk, :], NEG_INF)
            sel = past & (_topk_rank(g, bidx, nblk - 1, 0) < n_sel)
            bias_ref[h * nblk:(h + 1) * nblk, rows] = jnp.where(sel, 0.0, NEG_INF)


ROPE_BLOCKS = 4


def _rope_prompt(proj, n_seq, cos, sin):
    m = proj.shape[0]
    nblk = m // (n_seq * MOBA_BLOCK)
    n_step = nblk // ROPE_BLOCKS
    rows = ROPE_BLOCKS * MOBA_BLOCK
    row = lambda b, j: (b * n_step + j, 0)
    return pl.pallas_call(
        functools.partial(_rope_gate_kernel, nblk=nblk),
        grid=(n_seq, n_step),
        in_specs=[pl.BlockSpec((rows, D_C), lambda b, j: (b * n_step + j, C_Q // D_C)),
                  pl.BlockSpec((rows, D_C), lambda b, j: (b * n_step + j, C_K // D_C)),
                  pl.BlockSpec((rows, D_C), lambda b, j: (j, 0)),
                  pl.BlockSpec((rows, D_C), lambda b, j: (j, 0))],
        out_specs=[pl.BlockSpec((rows, D_C), row),
                   pl.BlockSpec((rows, D_C), row),
                   pl.BlockSpec((None, ATT_HEADS * nblk, rows), lambda b, j: (b, 0, j))],
        out_shape=[jax.ShapeDtypeStruct((m, D_C), F32), jax.ShapeDtypeStruct((m, D_C), F32),
                   jax.ShapeDtypeStruct((n_seq, ATT_HEADS * nblk, m // n_seq), F32)],
        scratch_shapes=[pltpu.VMEM((nblk, D_C), F32)],
        compiler_params=_cparams(("parallel", "arbitrary")),
        name="rope_prompt",
    )(proj, proj, cos, sin)


def _rope_sample(proj, cos, sin):
    m = proj.shape[0]
    return pl.pallas_call(
        _rope_kernel,
        grid=(1,),
        in_specs=[pl.BlockSpec((m, D_C), lambda i: (0, C_Q // D_C)),
                  pl.BlockSpec((m, D_C), lambda i: (0, C_K // D_C)),
                  pl.BlockSpec((m, D_C), lambda i: (0, 0)),
                  pl.BlockSpec((m, D_C), lambda i: (0, 0))],
        out_specs=[pl.BlockSpec((m, D_C), lambda i: (0, 0)), pl.BlockSpec((m, D_C), lambda i: (0, 0))],
        out_shape=[jax.ShapeDtypeStruct((m, D_C), F32), jax.ShapeDtypeStruct((m, D_C), F32)],
        compiler_params=_cparams(("arbitrary",)),
        name="rope_sample",
    )(proj, proj, cos, sin)


def _topk_rank(gate, idx, n_cand, axis):
    rank = jnp.zeros(gate.shape, jnp.int32)
    for i in range(n_cand):
        gi = gate[:, i:i + 1] if axis == 1 else gate[i:i + 1, :]
        beats = (gi > gate) | ((gi == gate) & (i < idx))
        rank = rank + jnp.where(beats, 1, 0)
    return rank


SUBLANES = 8
ONES_ROWS = 2 * SUBLANES


def _fold_rows(x, op):
    return op(x.reshape(x.shape[0] // SUBLANES, SUBLANES, x.shape[1]), axis=0)


def _moba_prompt_kernel(q_ref, k_ref, v_ref, bias_ref, o_ref, kb_ref, vt_ref, *, nblk):
    n_head = LANES // ATT_HEAD_DIM
    ones = jnp.ones((ONES_ROWS, MOBA_BLOCK), BF16)
    for j in range(nblk):
        kb_ref[j] = k_ref[j * MOBA_BLOCK:(j + 1) * MOBA_BLOCK, :].astype(BF16)
        vt_ref[j] = jnp.concatenate([v_ref[j * MOBA_BLOCK:(j + 1) * MOBA_BLOCK, :].T.astype(BF16), ones], axis=0)

    lane = lax.broadcasted_iota(jnp.int32, (MOBA_BLOCK, LANES), 1)
    own_bias = jnp.where(lax.broadcasted_iota(jnp.int32, (MOBA_BLOCK, MOBA_BLOCK), 0)
                         <= lax.broadcasted_iota(jnp.int32, (MOBA_BLOCK, MOBA_BLOCK), 1), 0.0, NEG_INF)
    row = lax.broadcasted_iota(jnp.int32, (LANES, MOBA_BLOCK), 0)

    def attend(n_past):
        qrows = slice(n_past * MOBA_BLOCK, (n_past + 1) * MOBA_BLOCK)
        q = q_ref[qrows, :] * (ATT_HEAD_DIM ** -0.5 * math.log2(math.e))
        qms = [jnp.where((lane >= h * ATT_HEAD_DIM) & (lane < (h + 1) * ATT_HEAD_DIM), q, 0.0).astype(BF16)
               for h in range(n_head)]
        n_all = n_past + 1

        def score_block(h, j):
            st = _dot_nt(kb_ref[j], qms[h])
            if j == n_past:
                st = st + own_bias
                return st, None, _fold_rows(st, jnp.max)
            bias = bias_ref[h * nblk + j:h * nblk + j + 1, qrows]
            return st, bias, _fold_rows(st, jnp.max) + bias

        def fold_max(blocks):
            mx = blocks[0][2]
            for blk in blocks[1:]:
                mx = jnp.maximum(mx, blk[2])
            return jnp.max(mx, axis=0, keepdims=True)

        def weigh(blk, m, j, acc):
            st, bias, _ = blk
            p = jnp.exp2(st - (m if bias is None else m - bias))
            pv = _dot(vt_ref[j], p.astype(BF16))
            return pv if acc is None else acc + pv

        accs = []
        for h in range(n_head):
            blocks = [score_block(h, j) for j in range(n_all)]
            m = fold_max(blocks)
            acc = None
            for j in range(n_all):
                acc = weigh(blocks[j], m, j, acc)
            accs.append(acc)
        outs = [acc[:LANES] / acc[LANES:LANES + 1] for acc in accs]
        o_ref[qrows, :] = jnp.where(row < ATT_HEAD_DIM, outs[0], outs[1]).T.astype(BF16)

    for n_past in range(nblk):
        attend(n_past)


def _moba_prompt(q_rot, k_rot, v_att, bias, n_seq):
    m = q_rot.shape[0]
    s_len = m // n_seq
    nblk = s_len // MOBA_BLOCK
    n_pair = D_C // LANES
    n_head = LANES // ATT_HEAD_DIM
    return pl.pallas_call(
        functools.partial(_moba_prompt_kernel, nblk=nblk),
        grid=(n_seq, n_pair),
        in_specs=[pl.BlockSpec((s_len, LANES), lambda b, hp: (b, hp)),
                  pl.BlockSpec((s_len, LANES), lambda b, hp: (b, hp)),
                  pl.BlockSpec((s_len, LANES), lambda b, hp: (b, hp)),
                  pl.BlockSpec((None, n_head * nblk, s_len), lambda b, hp: (b, hp, 0))],
        out_specs=pl.BlockSpec((s_len, LANES), lambda b, hp: (b, hp)),
        out_shape=jax.ShapeDtypeStruct((m, D_C), BF16),
        scratch_shapes=[pltpu.VMEM((nblk, MOBA_BLOCK, LANES), BF16),
                        pltpu.VMEM((nblk, LANES + ONES_ROWS, MOBA_BLOCK), BF16)],
        compiler_params=_cparams(("parallel", "parallel")),
        name="moba_prompt",
    )(q_rot, k_rot, v_att, bias)


SAMPLE_PAGE_BUFFERS = 32
SAMPLE_PAGE_UNROLL = 8


def _moba_sample_kernel(pt_ref, q_ref, kn_ref, vn_ref, ck_hbm, cv_hbm, o_ref,
                        buf_ref, sem, s_ref, own_ref, oacc_ref, *, n_seq, n_pages, t_len, layer):
    nbuf = SAMPLE_PAGE_BUFFERS
    per_seq = 2 * n_pages
    total = n_seq * per_seq
    scale = ATT_HEAD_DIM ** -0.5
    ppb = MOBA_BLOCK // PAGE_SIZE
    n_blk = n_pages // ppb
    n_row = ATT_HEADS * t_len
    rows = lambda h: slice(h * t_len, (h + 1) * t_len)

    def page_copy(src_hbm, page, slot):
        return pltpu.make_async_copy(src_hbm.at[page, layer], buf_ref.at[slot], sem.at[slot])

    def start_fetch(n, slot):
        b = n // per_seq
        i = n % per_seq

        @pl.when(i < n_pages)
        def _():
            page_copy(ck_hbm, pt_ref[b, i], slot).start()

        @pl.when(i >= n_pages)
        def _():
            page_copy(cv_hbm, pt_ref[b, i - n_pages], slot).start()

    def finish_fetch(slot):
        page_copy(ck_hbm, 0, slot).wait()

    def refill(n, slot):
        @pl.when(n + nbuf < total)
        def _():
            start_fetch(n + nbuf, slot)

    b = pl.program_id(0)

    @pl.when(b == 0)
    def _():
        for n in range(nbuf):
            start_fetch(n, n)

    def one_sequence():
        base = b * per_seq

        grp = SAMPLE_PAGE_UNROLL

        def score_pages(g, c):
            pages = [g * grp + u for u in range(grp)]
            for i in pages:
                finish_fetch(i % nbuf)
            for i in pages:
                for h in range(ATT_HEADS):
                    sc = _dot(q_ref[h].astype(BF16), buf_ref[i % nbuf, h].astype(BF16))
                    s_ref[i, rows(h), :] = sc * scale
            for i in pages:
                refill(base + i, i % nbuf)
            return c

        lax.fori_loop(0, n_pages // grp, score_pages, 0)
        select_and_normalise(q_ref, kn_ref, vn_ref)

        def weigh_pages(g, accs):
            pages = [g * grp + u for u in range(grp)]
            for i in pages:
                finish_fetch((n_pages + i) % nbuf)
            accs = tuple(
                accs[h] + _dot_nt(
                    jnp.concatenate([s_ref[i, rows(h), :] for i in pages], axis=1).astype(BF16),
                    jnp.concatenate([buf_ref[(n_pages + i) % nbuf, h] for i in pages], axis=1).astype(BF16))
                for h in range(ATT_HEADS))
            for i in pages:
                refill(base + n_pages + i, (n_pages + i) % nbuf)
            return accs

        accs = lax.fori_loop(0, n_pages // grp, weigh_pages, tuple(oacc_ref[h] for h in range(ATT_HEADS)))
        for h in range(ATT_HEADS):
            o_ref[h] = accs[h] * own_ref[rows(h), 0:1]

    def select_and_normalise(q_ref, kn_ref, vn_ref):
        lane = lax.broadcasted_iota(jnp.int32, (n_row, LANES), 1)
        gate = jnp.full((n_row, LANES), NEG_INF, F32)
        for j in range(n_blk):
            tot = s_ref[ppb * j]
            for i in range(1, ppb):
                tot = tot + s_ref[ppb * j + i]
            gate = jnp.where(lane == j, jnp.sum(tot, axis=1, keepdims=True), gate)
        sel = jnp.where(_topk_rank(gate, lane, n_blk, 1) < min(MOBA_TOPK, n_blk), 1, 0)
        zpad = jnp.zeros((PAGE_SIZE - t_len, ATT_HEAD_DIM), F32)
        for h in range(ATT_HEADS):
            kh = jnp.concatenate([kn_ref[:, h, :], zpad], axis=0).astype(BF16)
            own_ref[rows(h), :] = _dot_nt(q_ref[h].astype(BF16), kh) * scale
        tok = lax.broadcasted_iota(jnp.int32, (n_row, LANES), 0) % t_len
        so = jnp.where(lane <= tok, own_ref[...], NEG_INF)
        mv = so
        for j in range(n_blk):
            for i in range(ppb):
                mv = jnp.maximum(mv, jnp.where(sel[:, j:j + 1] > 0, s_ref[ppb * j + i], NEG_INF))
        m = jnp.max(mv, axis=1, keepdims=True)
        po = jnp.exp(so - m)
        own_ref[...] = po
        lv = po
        for j in range(n_blk):
            for i in range(ppb):
                pj = jnp.exp(jnp.where(sel[:, j:j + 1] > 0, s_ref[ppb * j + i], NEG_INF) - m)
                s_ref[ppb * j + i] = pj
                lv = lv + pj
        inv = 1.0 / jnp.sum(lv, axis=1, keepdims=True)
        for h in range(ATT_HEADS):
            vh = jnp.concatenate([vn_ref[:, h, :], zpad], axis=0).astype(BF16)
            oacc_ref[h] = _dot(own_ref[rows(h), :].astype(BF16), vh)
        own_ref[:, 0:1] = inv

    one_sequence()


def _moba_sample(q_htd, k_thd, v_thd, cache_kt, cache_vt, page_table, layer):
    n_seq, _, t_len, _ = q_htd.shape
    n_pages = page_table.shape[1]
    assert (2 * n_pages) % SAMPLE_PAGE_BUFFERS == 0 and n_pages % SAMPLE_PAGE_UNROLL == 0
    tok_shape = (n_seq, ATT_HEADS, t_len, ATT_HEAD_DIM)
    per_seq = lambda shape: pl.BlockSpec((None,) + shape, lambda b, pt: (b, 0, 0, 0))
    tok_blk = per_seq((ATT_HEADS, t_len, ATT_HEAD_DIM))
    new_blk = per_seq((t_len, ATT_HEADS, ATT_HEAD_DIM))
    grid_spec = pltpu.PrefetchScalarGridSpec(
        num_scalar_prefetch=1,
        grid=(n_seq,),
        in_specs=[tok_blk, new_blk, new_blk, pl.BlockSpec(memory_space=pl.ANY), pl.BlockSpec(memory_space=pl.ANY)],
        out_specs=tok_blk,
        scratch_shapes=[pltpu.VMEM((SAMPLE_PAGE_BUFFERS, ATT_HEADS, ATT_HEAD_DIM, PAGE_SIZE), F32),
                        pltpu.SemaphoreType.DMA((SAMPLE_PAGE_BUFFERS,)),
                        pltpu.VMEM((n_pages, ATT_HEADS * t_len, PAGE_SIZE), F32),
                        pltpu.VMEM((ATT_HEADS * t_len, PAGE_SIZE), F32),
                        pltpu.VMEM((ATT_HEADS, t_len, ATT_HEAD_DIM), F32)])
    return pl.pallas_call(
        functools.partial(_moba_sample_kernel, n_seq=n_seq, n_pages=n_pages, t_len=t_len, layer=layer),
        grid_spec=grid_spec,
        out_shape=jax.ShapeDtypeStruct(tok_shape, F32),
        compiler_params=_cparams(("arbitrary",)),
        name="moba_sample",
    )(page_table, q_htd, k_thd, v_thd, cache_kt, cache_vt)


def _merge_kernel(x_ref, ya_ref, yb_ref, yc_ref, g0_ref, g1_ref, g2_ref, bg_ref, wa_ref, wb_ref, wc_ref, wo_ref, o_ref):
    bg = bg_ref[...]
    merged = (jax.nn.sigmoid(g0_ref[...] + bg[:, :D_MODEL]) * _dot(ya_ref[...], wa_ref[...])
              + jax.nn.sigmoid(g1_ref[...] + bg[:, D_MODEL:2 * D_MODEL]) * _dot(yb_ref[...], wb_ref[...])
              + jax.nn.sigmoid(g2_ref[...] + bg[:, 2 * D_MODEL:]) * _dot(yc_ref[...], wc_ref[...]))
    o_ref[...] = x_ref[...] + _dot(merged.astype(BF16), wo_ref[...])


def _merge(x2d, ya, yb, yc, proj, b_gate, wa, wb, wc, wo, tm):
    m = x2d.shape[0]
    const = lambda i: (0, 0)
    gate_spec = lambda k: pl.BlockSpec((tm, D_MODEL), lambda i: (i, C_GATE // D_MODEL + k))
    return pl.pallas_call(
        _merge_kernel,
        grid=(m // tm,),
        in_specs=[pl.BlockSpec((tm, D_MODEL), lambda i: (i, 0)),
                  pl.BlockSpec((tm, D_A), lambda i: (i, 0)),
                  pl.BlockSpec((tm, D_B), lambda i: (i, 0)),
                  pl.BlockSpec((tm, D_C), lambda i: (i, 0)),
                  gate_spec(0), gate_spec(1), gate_spec(2),
                  pl.BlockSpec((1, N_BRANCH * D_MODEL), const),
                  pl.BlockSpec((D_A, D_MODEL), const),
                  pl.BlockSpec((D_B, D_MODEL), const),
                  pl.BlockSpec((D_C, D_MODEL), const),
                  pl.BlockSpec((D_MODEL, D_MODEL), const)],
        out_specs=pl.BlockSpec((tm, D_MODEL), lambda i: (i, 0)),
        out_shape=jax.ShapeDtypeStruct((m, D_MODEL), F32),
        compiler_params=_cparams(("parallel",)),
        name="merge",
    )(x2d, ya, yb, yc, proj, proj, proj, b_gate, wa, wb, wc, wo)


def _ffn_kernel(x_ref, g_ref, wu_ref, wd_ref, gf_ref, o_ref, h_ref, acc_ref, *, final_norm):
    j = pl.program_id(1)

    @pl.when(j == 0)
    def _():
        h_ref[...] = _rms(x_ref[...], g_ref[...]).astype(BF16)
        acc_ref[...] = x_ref[...]

    f = jnp.maximum(_dot(h_ref[...], wu_ref[...]), 0.0)
    acc_ref[...] += _dot((f * f).astype(BF16), wd_ref[...])

    @pl.when(j == pl.num_programs(1) - 1)
    def _():
        y = acc_ref[...]
        o_ref[...] = _rms(y, gf_ref[...]) if final_norm else y


def _ffn(x2d, g, wu, wd, gf, tm, tf, final_norm):
    m = x2d.shape[0]
    return pl.pallas_call(
        functools.partial(_ffn_kernel, final_norm=final_norm),
        grid=(m // tm, D_FF // tf),
        in_specs=[pl.BlockSpec((tm, D_MODEL), lambda i, j: (i, 0)),
                  pl.BlockSpec((1, D_MODEL), lambda i, j: (0, 0)),
                  pl.BlockSpec((D_MODEL, tf), lambda i, j: (0, j)),
                  pl.BlockSpec((tf, D_MODEL), lambda i, j: (j, 0)),
                  pl.BlockSpec((1, D_MODEL), lambda i, j: (0, 0))],
        out_specs=pl.BlockSpec((tm, D_MODEL), lambda i, j: (i, 0)),
        out_shape=jax.ShapeDtypeStruct((m, D_MODEL), F32),
        scratch_shapes=[pltpu.VMEM((tm, D_MODEL), BF16), pltpu.VMEM((tm, D_MODEL), F32)],
        compiler_params=_cparams(("parallel", "arbitrary")),
        name="ffn",
    )(x2d, g, wu, wd, gf)


def _prep_layer(l, norm1_g, w_in, b_gate, ln_v_g, ln_v_b, w_spatial, b_spatial, w_a_out, conv_w, conv_b,
                dt_bias, a_log, d_skip, ssm_norm_g, w_b_out, w_c_out, w_o, norm2_g, w_up, w_down, t_len):
    w = w_in[l]
    o_dt = 2 * D_A + D_B + CONV_DIM
    o_q = o_dt + SSM_HEADS
    o_gate = o_q + 3 * D_C
    w_perm = jnp.concatenate(
        [w[:, :o_dt], w[:, o_gate:], w[:, o_q:o_gate], w[:, o_dt:o_q],
         jnp.zeros((D_MODEL, DT_W - SSM_HEADS), w.dtype)], axis=1).astype(BF16)
    tril = jnp.tril(jnp.ones((A_CHUNK, A_CHUNK), bool))
    w_tril = jnp.where(tril[None], w_spatial[l], 0)
    bs_exp = jnp.repeat(b_spatial[l].T, A_GROUP_DIM, axis=1)
    w_exp = jnp.repeat(jnp.transpose(w_tril[:, :t_len, :t_len], (2, 1, 0)), A_GROUP_DIM, axis=2)
    pad_h = lambda v: jnp.concatenate([v, jnp.zeros((LANES - SSM_HEADS,), v.dtype)])[None, :]
    return dict(
        norm1_g=norm1_g[l][None, :], w_in=w_perm, b_gate=b_gate[l][None, :],
        ln_g=ln_v_g[l][None, :], ln_b=ln_v_b[l][None, :],
        w_tril=w_tril.astype(BF16), bs_exp=bs_exp, w_exp=w_exp, bs_exp_s=bs_exp[:t_len],
        conv_w=conv_w[l], conv_b=conv_b[l][None, :], dt_bias=pad_h(dt_bias[l]), a_log=pad_h(a_log[l]),
        d_skip=jnp.repeat(d_skip[l], SSM_HEAD_DIM)[None, :], ssm_norm_g=ssm_norm_g[l][None, :],
        w_a_out=w_a_out[l].astype(BF16), w_b_out=w_b_out[l].astype(BF16), w_c_out=w_c_out[l].astype(BF16),
        w_o=w_o[l].astype(BF16), norm2_g=norm2_g[l][None, :],
        w_up=w_up[l].astype(BF16), w_down=w_down[l].astype(BF16))


def _rope_tables(pos):
    inv = jnp.power(jnp.float32(ROPE_THETA), -jnp.arange(0, ATT_HEAD_DIM, 2, dtype=F32) / ATT_HEAD_DIM)
    ang = pos.astype(F32)[:, None] * inv[None, :]
    cos, sin = jnp.cos(ang), jnp.sin(ang)
    cos_h = jnp.concatenate([cos, cos], axis=1)
    sin_h = jnp.concatenate([-sin, sin], axis=1)
    return jnp.tile(cos_h, (1, ATT_HEADS)), jnp.tile(sin_h, (1, ATT_HEADS))


def kernel(x_prompt, x_sample, cache_k, cache_v, state_ssm, state_conv, page_table, norm1_g, w_in, b_gate, ln_v_g, ln_v_b, w_spatial, b_spatial, w_a_out, conv_w, conv_b, dt_bias, a_log, d_skip, ssm_norm_g, w_b_out, w_c_out, w_o, norm2_g, w_up, w_down, norm_f_g):
    bp, s_len, _ = x_prompt.shape
    db, t_len, _ = x_sample.shape
    depth = w_in.shape[0]
    past_len = page_table.shape[1] * PAGE_SIZE
    mp, ms = bp * s_len, db * t_len
    cos_p, sin_p = _rope_tables(jnp.arange(s_len))
    cos_s, sin_s = _rope_tables(past_len + jnp.arange(t_len))
    cos_s, sin_s = jnp.tile(cos_s, (db, 1)), jnp.tile(sin_s, (db, 1))
    conv_zero = jnp.zeros((1, bp, SSM_CONV - 1, CONV_DIM), F32)
    h_zero = jnp.zeros((1, bp, D_B, SSM_STATE), F32)
    gf = norm_f_g[None, :]
    tm_p = 1024
    cache_kt = jnp.transpose(cache_k, (0, 1, 3, 4, 2))
    cache_vt = jnp.transpose(cache_v, (0, 1, 3, 4, 2))

    xp = x_prompt.reshape(mp, D_MODEL)
    xs = x_sample.reshape(ms, D_MODEL)
    outs = {k: [] for k in ("a_v_s", "ssm_p", "ssm_s", "conv_p", "conv_s", "k_p", "v_p", "k_s", "v_s")}
    for l in range(depth):
        w = _prep_layer(l, norm1_g, w_in, b_gate, ln_v_g, ln_v_b, w_spatial, b_spatial, w_a_out, conv_w, conv_b,
                        dt_bias, a_log, d_skip, ssm_norm_g, w_b_out, w_c_out, w_o, norm2_g, w_up, w_down, t_len)
        last = l == depth - 1

        proj, v_att = _in_proj(xp, w["norm1_g"], w["w_in"], tm_p)
        ya = _branch_a_prompt(proj, w["ln_g"], w["ln_b"], w["w_tril"], w["bs_exp"])
        yb, h_new = _branch_b(proj, bp, SSM_CHUNK, conv_zero, h_zero, 0, w["conv_w"], w["conv_b"], w["dt_bias"],
                              w["a_log"], w["d_skip"], w["ssm_norm_g"])
        q_rot, k_rot, sel_bias = _rope_prompt(proj, bp, cos_p, sin_p)
        yc = _moba_prompt(q_rot, k_rot, v_att, sel_bias, bp)
        x1 = _merge(xp, ya, yb, yc, proj, w["b_gate"], w["w_a_out"], w["w_b_out"], w["w_c_out"], w["w_o"], 512)
        xp = _ffn(x1, w["norm2_g"], w["w_up"], w["w_down"], gf, tm_p, 1024, last)
        proj3 = proj.reshape(bp, s_len, N_COLS)
        outs["ssm_p"].append(h_new.reshape(bp, SSM_HEADS, SSM_HEAD_DIM, SSM_STATE))
        outs["conv_p"].append(proj3[:, s_len - (SSM_CONV - 1):, C_XBC:C_XBC + CONV_DIM])
        outs["k_p"].append(k_rot.reshape(bp, s_len, ATT_HEADS, ATT_HEAD_DIM))
        outs["v_p"].append(v_att.reshape(bp, s_len, ATT_HEADS, ATT_HEAD_DIM))

        proj, v_att = _in_proj(xs, w["norm1_g"], w["w_in"], ms)
        ya, vn = _branch_a_sample(proj, t_len, w["ln_g"], w["ln_b"], w["w_exp"], w["bs_exp_s"])
        yb, h_new = _branch_b(proj, db, t_len, state_conv, state_ssm.reshape(depth, db, D_B, SSM_STATE), l,
                              w["conv_w"], w["conv_b"], w["dt_bias"], w["a_log"], w["d_skip"], w["ssm_norm_g"])
        q_rot, k_rot = _rope_sample(proj, cos_s, sin_s)
        proj3 = proj.reshape(db, t_len, N_COLS)
        v_new = v_att.reshape(db, t_len, ATT_HEADS, ATT_HEAD_DIM)
        k_new = k_rot.reshape(db, t_len, ATT_HEADS, ATT_HEAD_DIM)
        q_htd = jnp.transpose(q_rot.reshape(db, t_len, ATT_HEADS, ATT_HEAD_DIM), (0, 2, 1, 3))
        o_htd = _moba_sample(q_htd, k_new, v_new, cache_kt, cache_vt, page_table, l)
        yc = jnp.transpose(o_htd, (0, 2, 1, 3)).reshape(ms, D_C).astype(BF16)
        x1 = _merge(xs, ya, yb, yc, proj, w["b_gate"], w["w_a_out"], w["w_b_out"], w["w_c_out"], w["w_o"], ms)
        xs = _ffn(x1, w["norm2_g"], w["w_up"], w["w_down"], gf, ms, 1024, last)
        outs["a_v_s"].append(vn.reshape(db, t_len, D_A))
        outs["ssm_s"].append(h_new.reshape(db, SSM_HEADS, SSM_HEAD_DIM, SSM_STATE))
        if t_len >= SSM_CONV - 1:
            conv_new = proj3[:, t_len - (SSM_CONV - 1):, C_XBC:C_XBC + CONV_DIM]
        else:
            conv_new = jnp.concatenate([state_conv[l], proj3[:, :, C_XBC:C_XBC + CONV_DIM]],
                                       axis=1)[:, -(SSM_CONV - 1):]
        outs["conv_s"].append(conv_new)
        outs["k_s"].append(k_new)
        outs["v_s"].append(v_new)

    st = lambda k: jnp.stack(outs[k])
    return (xp.reshape(bp, s_len, D_MODEL), xs.reshape(db, t_len, D_MODEL), st("a_v_s"), st("ssm_p"), st("ssm_s"),
            st("conv_p"), st("conv_s"), st("k_p"), st("v_p"), st("k_s"), st("v_s"))
```

```python
import functools
import math

import jax
import jax.numpy as jnp
from jax import lax
from jax.experimental import pallas as pl
from jax.experimental.pallas import tpu as pltpu

F32 = jnp.float32
BF16 = jnp.bfloat16
NEG_INF = float("-inf")

D_MODEL = 1024
PAGE_SIZE = 128
A_CHUNK = 128
D_A = D_MODEL // 2
A_GROUPS = 8
A_GROUP_DIM = D_A // A_GROUPS
D_B = D_MODEL
SSM_HEAD_DIM = 64
SSM_HEADS = D_B // SSM_HEAD_DIM
SSM_GROUPS = 4
SSM_STATE = 128
SSM_CONV = 4
SSM_CHUNK = 128
CONV_DIM = D_B + 2 * SSM_GROUPS * SSM_STATE
ATT_HEADS = 8
ATT_HEAD_DIM = 64
D_C = ATT_HEADS * ATT_HEAD_DIM
MOBA_BLOCK = 256
MOBA_TOPK = 3
ROPE_THETA = 10000.0
N_BRANCH = 3
D_FF = 4 * D_MODEL
EPS = 1e-6

LANES = 128
C_U, C_V, C_Z, C_XBC, C_GATE, C_Q, C_K, C_VATT, C_DT = 0, 512, 1024, 2048, 4096, 7168, 7680, 8192, 8704
DT_W = 256
N_COLS = C_DT + DT_W
IN_TN = 1280
VMEM_LIMIT = 56 * 1024 * 1024


def _cparams(sem):
    return pltpu.CompilerParams(dimension_semantics=sem, vmem_limit_bytes=VMEM_LIMIT)


def _dot(a, b):
    return jnp.dot(a, b, preferred_element_type=F32)


def _dot_nt(a, b):
    return lax.dot_general(a, b, (((1,), (1,)), ((), ())), preferred_element_type=F32)


def _split3(x):
    x1 = x.astype(BF16)
    r = x - x1.astype(F32)
    x2 = r.astype(BF16)
    r = r - x2.astype(F32)
    return x1, x2, r.astype(BF16)


def _dot_hi_nt(a, b):
    a1, a2, a3 = _split3(a)
    b1, b2, b3 = _split3(b)
    return (_dot_nt(a1, b1) + (_dot_nt(a1, b2) + _dot_nt(a2, b1))
            + (_dot_nt(a1, b3) + _dot_nt(a2, b2) + _dot_nt(a3, b1)))


def _rms(x, g):
    return x * lax.rsqrt(jnp.mean(x * x, axis=-1, keepdims=True) + EPS) * g


def _gelu(x):
    return 0.5 * x * (1.0 + lax.erf(x * math.sqrt(0.5)))


def _silu(x):
    return x * jax.nn.sigmoid(x)


def _inproj_kernel(x_ref, g_ref, w_ref, o_ref, v_ref, h_ref):
    j = pl.program_id(1)

    @pl.when(j == 0)
    def _():
        h_ref[...] = _rms(x_ref[...], g_ref[...]).astype(BF16)

    o_ref[...] = _dot(h_ref[...], w_ref[...])

    @pl.when(j == C_VATT // IN_TN)
    def _():
        v_ref[...] = o_ref[:, C_VATT % IN_TN:C_VATT % IN_TN + D_C]


def _in_proj(x2d, g, w_bf, tm):
    m = x2d.shape[0]
    assert C_VATT // IN_TN == (C_VATT + D_C - 1) // IN_TN
    return pl.pallas_call(
        _inproj_kernel,
        grid=(m // tm, N_COLS // IN_TN),
        in_specs=[pl.BlockSpec((tm, D_MODEL), lambda i, j: (i, 0)),
                  pl.BlockSpec((1, D_MODEL), lambda i, j: (0, 0)),
                  pl.BlockSpec((D_MODEL, IN_TN), lambda i, j: (0, j))],
        out_specs=[pl.BlockSpec((tm, IN_TN), lambda i, j: (i, j)),
                   pl.BlockSpec((tm, D_C), lambda i, j: (i, 0))],
        out_shape=[jax.ShapeDtypeStruct((m, N_COLS), F32), jax.ShapeDtypeStruct((m, D_C), F32)],
        scratch_shapes=[pltpu.VMEM((tm, D_MODEL), BF16)],
        compiler_params=_cparams(("parallel", "arbitrary")),
        name="in_proj",
    )(x2d, g, w_bf)


def _layernorm(x, g, b):
    xc = x - jnp.mean(x, axis=-1, keepdims=True)
    return xc * lax.rsqrt(jnp.mean(xc * xc, axis=-1, keepdims=True) + EPS) * g + b


A_ROWS = 8 * A_CHUNK


def _branch_a_kernel(u_ref, v_ref, g_ref, b_ref, w_ref, bs_ref, ya_ref):
    first = lax.broadcasted_iota(jnp.int32, (A_CHUNK, LANES), 1) < A_GROUP_DIM
    for c in range(A_ROWS // A_CHUNK):
        rows = slice(c * A_CHUNK, (c + 1) * A_CHUNK)
        gu = _gelu(u_ref[rows, :])
        vn = _layernorm(_gelu(v_ref[rows, :]), g_ref[...], b_ref[...]).astype(BF16)
        parts = []
        for p in range(A_GROUPS // 2):
            vp = vn[:, p * LANES:(p + 1) * LANES]
            parts.append(jnp.where(first, _dot(w_ref[2 * p], vp), _dot(w_ref[2 * p + 1], vp)))
        mixed = jnp.concatenate(parts, axis=1) + bs_ref[...]
        ya_ref[rows, :] = (gu * mixed).astype(BF16)


def _branch_a_prompt(proj, ln_g, ln_b, w_tril_bf, bs_exp):
    m = proj.shape[0]
    return pl.pallas_call(
        _branch_a_kernel,
        grid=(m // A_ROWS,),
        in_specs=[pl.BlockSpec((A_ROWS, D_A), lambda i: (i, C_U // D_A)),
                  pl.BlockSpec((A_ROWS, D_A), lambda i: (i, C_V // D_A)),
                  pl.BlockSpec((1, D_A), lambda i: (0, 0)),
                  pl.BlockSpec((1, D_A), lambda i: (0, 0)),
                  pl.BlockSpec((A_GROUPS, A_CHUNK, A_CHUNK), lambda i: (0, 0, 0)),
                  pl.BlockSpec((A_CHUNK, D_A), lambda i: (0, 0))],
        out_specs=pl.BlockSpec((A_ROWS, D_A), lambda i: (i, 0)),
        out_shape=jax.ShapeDtypeStruct((m, D_A), BF16),
        compiler_params=_cparams(("parallel",)),
        name="branch_a_prompt",
    )(proj, proj, ln_g, ln_b, w_tril_bf, bs_exp)


def _branch_a_sample_kernel(u_ref, v_ref, g_ref, b_ref, wexp_ref, bs_ref, ya_ref, vn_ref):
    t_len = bs_ref.shape[0]
    gu = _gelu(u_ref[...])
    vn = _layernorm(_gelu(v_ref[...]), g_ref[...], b_ref[...])
    vn_ref[...] = vn
    vn3 = vn.reshape(vn.shape[0] // t_len, t_len, D_A)
    mixed = jnp.broadcast_to(bs_ref[...][None], vn3.shape)
    for s in range(t_len):
        mixed = mixed + wexp_ref[s][None] * vn3[:, s:s + 1, :]
    ya_ref[...] = (gu * mixed.reshape(vn.shape)).astype(BF16)


def _branch_a_sample(proj, t_len, ln_g, ln_b, w_exp, bs_exp):
    m = proj.shape[0]
    return pl.pallas_call(
        _branch_a_sample_kernel,
        grid=(1,),
        in_specs=[pl.BlockSpec((m, D_A), lambda i: (0, C_U // D_A)),
                  pl.BlockSpec((m, D_A), lambda i: (0, C_V // D_A)),
                  pl.BlockSpec((1, D_A), lambda i: (0, 0)),
                  pl.BlockSpec((1, D_A), lambda i: (0, 0)),
                  pl.BlockSpec((t_len, t_len, D_A), lambda i: (0, 0, 0)),
                  pl.BlockSpec((t_len, D_A), lambda i: (0, 0))],
        out_specs=[pl.BlockSpec((m, D_A), lambda i: (0, 0)),
                   pl.BlockSpec((m, D_A), lambda i: (0, 0))],
        out_shape=[jax.ShapeDtypeStruct((m, D_A), BF16), jax.ShapeDtypeStruct((m, D_A), F32)],
        compiler_params=_cparams(("arbitrary",)),
        name="branch_a_sample",
    )(proj, proj, ln_g, ln_b, w_exp, bs_exp)


XP_OFF = 8
SSD_CHUNKS_PER_STEP = 4


def _ssd_kernel(*refs, rows, chunks):
    for cc in range(chunks):
        _ssd_chunk(pl.program_id(1) * chunks + cc, slice(cc * rows, (cc + 1) * rows), *refs, rows=rows)


def _ssd_chunk(c, rs, z_ref, xbc_ref, dt_ref, cprev_ref, h0_ref, cw_ref, cb_ref, dtb_ref, alog_ref, dsk_ref, ng_ref,
               yb_ref, hout_ref, xp_ref, *, rows):
    t_len = SSM_CHUNK
    xbc = xbc_ref[rs, :]

    def pad(a):
        if rows == t_len:
            return a
        return jnp.concatenate([a, jnp.zeros((t_len - rows, a.shape[1]), a.dtype)], axis=0)

    @pl.when(c == 0)
    def _():
        xp_ref[XP_OFF - 3:XP_OFF, :] = cprev_ref[...]
        hout_ref[...] = h0_ref[...]

    @pl.when(c > 0)
    def _():
        xp_ref[XP_OFF - 3:XP_OFF, :] = xp_ref[XP_OFF + rows - 3:XP_OFF + rows, :]

    xp_ref[XP_OFF:XP_OFF + rows, :] = xbc
    acc = cb_ref[...] + xbc * cw_ref[SSM_CONV - 1:SSM_CONV, :]
    for tap in range(SSM_CONV - 1):
        lo = XP_OFF - (SSM_CONV - 1) + tap
        acc = acc + xp_ref[lo:lo + rows, :] * cw_ref[tap:tap + 1, :]
    act = _silu(acc)
    xs_rows = act[:, :D_B]
    xs = pad(xs_rows)
    bm = pad(act[:, D_B:D_B + SSM_GROUPS * SSM_STATE]).astype(BF16)
    cm = pad(act[:, D_B + SSM_GROUPS * SSM_STATE:]).astype(BF16)

    dt = pad(jax.nn.softplus(dt_ref[rs, :][:, :LANES] + dtb_ref[...]))
    row_i = lax.broadcasted_iota(jnp.int32, (t_len, t_len), 0)
    col_i = lax.broadcasted_iota(jnp.int32, (t_len, t_len), 1)
    a = dt * (-jnp.exp(alog_ref[...]))
    tril = jnp.where(col_i <= row_i, 1.0, 0.0).astype(BF16)
    a1, a2, a3 = _split3(a)
    acs = _dot(tril, a1) + _dot(tril, a2) + _dot(tril, a3)
    acs_t = acs.T
    dt_t = dt.T
    xs_t = xs.T
    causal_t = row_i <= col_i
    rep = SSM_HEADS // SSM_GROUPS
    ys = []
    for g in range(SSM_GROUPS):
        bg = bm[:, g * SSM_STATE:(g + 1) * SSM_STATE]
        cg = cm[:, g * SSM_STATE:(g + 1) * SSM_STATE]
        cb_t = _dot_nt(bg, cg)
        for r in range(rep):
            h = g * rep + r
            hs = slice(h * SSM_HEAD_DIM, (h + 1) * SSM_HEAD_DIM)
            acs_row = acs_t[h:h + 1, :]
            acs_col = acs[:, h:h + 1]
            dec_t = jnp.exp(jnp.where(causal_t, acs_row - acs_col, NEG_INF))
            w_t = (cb_t * dec_t).astype(BF16)
            xdt_t = xs_t[hs, :] * dt_t[h:h + 1, :]
            h_in = hout_ref[hs, :]
            y_t = _dot(xdt_t.astype(BF16), w_t) + _dot_nt(h_in.astype(BF16), cg) * jnp.exp(acs_row)
            last = acs_row[:, t_len - 1:t_len]
            st = _dot((xdt_t * jnp.exp(last - acs_row)).astype(BF16), bg)
            hout_ref[hs, :] = jnp.exp(last) * h_in + st
            ys.append(y_t)
    y = jnp.concatenate(ys, axis=0).T[:rows] + dsk_ref[...] * xs_rows
    yb_ref[rs, :] = _rms(y * _silu(z_ref[rs, :]), ng_ref[...]).astype(BF16)


def _branch_b(proj, n_seq, rows, conv_prev, h0, layer, conv_w, conv_b, dt_bias_p, a_log_p, d_skip_exp, norm_g):
    m = proj.shape[0]
    n_chunk = m // (n_seq * rows)
    chunks = SSD_CHUNKS_PER_STEP if n_chunk % SSD_CHUNKS_PER_STEP == 0 else 1
    n_step = n_chunk // chunks
    blk = rows * chunks
    const = lambda b, c: (0, 0)
    return pl.pallas_call(
        functools.partial(_ssd_kernel, rows=rows, chunks=chunks),
        grid=(n_seq, n_step),
        in_specs=[pl.BlockSpec((blk, D_B), lambda b, c: (b * n_step + c, C_Z // D_B)),
                  pl.BlockSpec((blk, CONV_DIM), lambda b, c: (b * n_step + c, C_XBC // CONV_DIM)),
                  pl.BlockSpec((blk, DT_W), lambda b, c: (b * n_step + c, C_DT // DT_W)),
                  pl.BlockSpec((None, None, SSM_CONV - 1, CONV_DIM), lambda b, c: (layer, b, 0, 0)),
                  pl.BlockSpec((None, None, D_B, SSM_STATE), lambda b, c: (layer, b, 0, 0)),
                  pl.BlockSpec((SSM_CONV, CONV_DIM), const),
                  pl.BlockSpec((1, CONV_DIM), const),
                  pl.BlockSpec((1, LANES), const),
                  pl.BlockSpec((1, LANES), const),
                  pl.BlockSpec((1, D_B), const),
                  pl.BlockSpec((1, D_B), const)],
        out_specs=[pl.BlockSpec((blk, D_B), lambda b, c: (b * n_step + c, 0)),
                   pl.BlockSpec((None, D_B, SSM_STATE), lambda b, c: (b, 0, 0))],
        out_shape=[jax.ShapeDtypeStruct((m, D_B), BF16),
                   jax.ShapeDtypeStruct((n_seq, D_B, SSM_STATE), F32)],
        scratch_shapes=[pltpu.VMEM((XP_OFF + SSM_CHUNK, CONV_DIM), F32)],
        compiler_params=_cparams(("parallel", "arbitrary")),
        name="branch_b_rows%d" % rows,
    )(proj, proj, proj, conv_prev, h0, conv_w, conv_b, dt_bias_p, a_log_p, d_skip_exp, norm_g)


def _rope_apply(x, cos, sin_signed, first):
    partner = jnp.where(first, pltpu.roll(x, D_C - ATT_HEAD_DIM // 2, 1), pltpu.roll(x, ATT_HEAD_DIM // 2, 1))
    return x * cos + partner * sin_signed


def _rope_kernel(q_ref, k_ref, cos_ref, sin_ref, qo_ref, ko_ref):
    lane = lax.broadcasted_iota(jnp.int32, q_ref.shape, 1)
    first = (lane % ATT_HEAD_DIM) < ATT_HEAD_DIM // 2
    cos, sin = cos_ref[...], sin_ref[...]
    qo_ref[...] = _rope_apply(q_ref[...], cos, sin, first)
    ko_ref[...] = _rope_apply(k_ref[...], cos, sin, first)


def _rope_gate_kernel(q_ref, k_ref, cos_ref, sin_ref, qo_ref, ko_ref, bias_ref, km_ref, *, nblk):
    step = pl.program_id(1)
    lane = lax.broadcasted_iota(jnp.int32, (MOBA_BLOCK, D_C), 1)
    first = (lane % ATT_HEAD_DIM) < ATT_HEAD_DIM // 2
    klane = lax.broadcasted_iota(jnp.int32, (nblk, D_C), 1) // ATT_HEAD_DIM
    bidx = lax.broadcasted_iota(jnp.int32, (nblk, MOBA_BLOCK), 0)
    n_sel = max(1, min(MOBA_TOPK, nblk - 1))

    @pl.when(step == 0)
    def _():
        km_ref[...] = jnp.zeros(km_ref.shape, F32)

    for r in range(ROPE_BLOCKS):
        j = step * ROPE_BLOCKS + r
        rows = slice(r * MOBA_BLOCK, (r + 1) * MOBA_BLOCK)
        cos, sin = cos_ref[rows, :], sin_ref[rows, :]
        qr = _rope_apply(q_ref[rows, :], cos, sin, first)
        kr = _rope_apply(k_ref[rows, :], cos, sin, first)
        qo_ref[rows, :] = qr
        ko_ref[rows, :] = kr
        km = km_ref[...]
        km_ref[pl.ds(j, 1), :] = jnp.mean(kr, axis=0, keepdims=True)
        kstack = jnp.concatenate([jnp.where(klane == h, km, 0.0) for h in range(ATT_HEADS)], axis=0)
        gate = _dot_hi_nt(kstack, qr)
        past = bidx < j
        for h in range(ATT_HEADS):
            g = jnp.where(past, gate[h * nblk:(h + 1) * nblk, :], NEG_INF)
            sel = past & (_topk_rank(g, bidx, nblk - 1, 0) < n_sel)
            bias_ref[h * nblk:(h + 1) * nblk, rows] = jnp.where(sel, 0.0, NEG_INF)


ROPE_BLOCKS = 4


def _rope_prompt(proj, n_seq, cos, sin):
    m = proj.shape[0]
    nblk = m // (n_seq * MOBA_BLOCK)
    n_step = nblk // ROPE_BLOCKS
    rows = ROPE_BLOCKS * MOBA_BLOCK
    row = lambda b, j: (b * n_step + j, 0)
    return pl.pallas_call(
        functools.partial(_rope_gate_kernel, nblk=nblk),
        grid=(n_seq, n_step),
        in_specs=[pl.BlockSpec((rows, D_C), lambda b, j: (b * n_step + j, C_Q // D_C)),
                  pl.BlockSpec((rows, D_C), lambda b, j: (b * n_step + j, C_K // D_C)),
                  pl.BlockSpec((rows, D_C), lambda b, j: (j, 0)),
                  pl.BlockSpec((rows, D_C), lambda b, j: (j, 0))],
        out_specs=[pl.BlockSpec((rows, D_C), row),
                   pl.BlockSpec((rows, D_C), row),
                   pl.BlockSpec((None, ATT_HEADS * nblk, rows), lambda b, j: (b, 0, j))],
        out_shape=[jax.ShapeDtypeStruct((m, D_C), F32), jax.ShapeDtypeStruct((m, D_C), F32),
                   jax.ShapeDtypeStruct((n_seq, ATT_HEADS * nblk, m // n_seq), F32)],
        scratch_shapes=[pltpu.VMEM((nblk, D_C), F32)],
        compiler_params=_cparams(("parallel", "arbitrary")),
        name="rope_prompt",
    )(proj, proj, cos, sin)


def _rope_sample(proj, cos, sin):
    m = proj.shape[0]
    return pl.pallas_call(
        _rope_kernel,
        grid=(1,),
        in_specs=[pl.BlockSpec((m, D_C), lambda i: (0, C_Q // D_C)),
                  pl.BlockSpec((m, D_C), lambda i: (0, C_K // D_C)),
                  pl.BlockSpec((m, D_C), lambda i: (0, 0)),
                  pl.BlockSpec((m, D_C), lambda i: (0, 0))],
        out_specs=[pl.BlockSpec((m, D_C), lambda i: (0, 0)), pl.BlockSpec((m, D_C), lambda i: (0, 0))],
        out_shape=[jax.ShapeDtypeStruct((m, D_C), F32), jax.ShapeDtypeStruct((m, D_C), F32)],
        compiler_params=_cparams(("arbitrary",)),
        name="rope_sample",
    )(proj, proj, cos, sin)


def _topk_rank(gate, idx, n_cand, axis):
    rank = jnp.zeros(gate.shape, jnp.int32)
    for i in range(n_cand):
        gi = gate[:, i:i + 1] if axis == 1 else gate[i:i + 1, :]
        beats = (gi > gate) | ((gi == gate) & (i < idx))
        rank = rank + jnp.where(beats, 1, 0)
    return rank


SUBLANES = 8
ONES_ROWS = 2 * SUBLANES


def _fold_rows(x, op):
    return op(x.reshape(x.shape[0] // SUBLANES, SUBLANES, x.shape[1]), axis=0)


def _moba_prompt_kernel(q_ref, k_ref, v_ref, bias_ref, o_ref, kb_ref, vt_ref, *, nblk):
    n_head = LANES // ATT_HEAD_DIM
    ones = jnp.ones((ONES_ROWS, MOBA_BLOCK), BF16)
    for j in range(nblk):
        kb_ref[j] = k_ref[j * MOBA_BLOCK:(j + 1) * MOBA_BLOCK, :].astype(BF16)
        vt_ref[j] = jnp.concatenate([v_ref[j * MOBA_BLOCK:(j + 1) * MOBA_BLOCK, :].T.astype(BF16), ones], axis=0)

    lane = lax.broadcasted_iota(jnp.int32, (MOBA_BLOCK, LANES), 1)
    own_bias = jnp.where(lax.broadcasted_iota(jnp.int32, (MOBA_BLOCK, MOBA_BLOCK), 0)
                         <= lax.broadcasted_iota(jnp.int32, (MOBA_BLOCK, MOBA_BLOCK), 1), 0.0, NEG_INF)
    row = lax.broadcasted_iota(jnp.int32, (LANES, MOBA_BLOCK), 0)

    def attend(n_past):
        qrows = slice(n_past * MOBA_BLOCK, (n_past + 1) * MOBA_BLOCK)
        q = q_ref[qrows, :] * (ATT_HEAD_DIM ** -0.5 * math.log2(math.e))
        qms = [jnp.where((lane >= h * ATT_HEAD_DIM) & (lane < (h + 1) * ATT_HEAD_DIM), q, 0.0).astype(BF16)
               for h in range(n_head)]
        n_all = n_past + 1

        def score_block(h, j):
            st = _dot_nt(kb_ref[j], qms[h])
            if j == n_past:
                st = st + own_bias
                return st, None, _fold_rows(st, jnp.max)
            bias = bias_ref[h * nblk + j:h * nblk + j + 1, qrows]
            return st, bias, _fold_rows(st, jnp.max) + bias

        def fold_max(blocks):
            mx = blocks[0][2]
            for blk in blocks[1:]:
                mx = jnp.maximum(mx, blk[2])
            return jnp.max(mx, axis=0, keepdims=True)

        def weigh(blk, m, j, acc):
            st, bias, _ = blk
            p = jnp.exp2(st - (m if bias is None else m - bias))
            pv = _dot(vt_ref[j], p.astype(BF16))
            return pv if acc is None else acc + pv

        accs = []
        for h in range(n_head):
            blocks = [score_block(h, j) for j in range(n_all)]
            m = fold_max(blocks)
            acc = None
            for j in range(n_all):
                acc = weigh(blocks[j], m, j, acc)
            accs.append(acc)
        outs = [acc[:LANES] / acc[LANES:LANES + 1] for acc in accs]
        o_ref[qrows, :] = jnp.where(row < ATT_HEAD_DIM, outs[0], outs[1]).T.astype(BF16)

    for n_past in range(nblk):
        attend(n_past)


def _moba_prompt(q_rot, k_rot, v_att, bias, n_seq):
    m = q_rot.shape[0]
    s_len = m // n_seq
    nblk = s_len // MOBA_BLOCK
    n_pair = D_C // LANES
    n_head = LANES // ATT_HEAD_DIM
    return pl.pallas_call(
        functools.partial(_moba_prompt_kernel, nblk=nblk),
        grid=(n_seq, n_pair),
        in_specs=[pl.BlockSpec((s_len, LANES), lambda b, hp: (b, hp)),
                  pl.BlockSpec((s_len, LANES), lambda b, hp: (b, hp)),
                  pl.BlockSpec((s_len, LANES), lambda b, hp: (b, hp)),
                  pl.BlockSpec((None, n_head * nblk, s_len), lambda b, hp: (b, hp, 0))],
        out_specs=pl.BlockSpec((s_len, LANES), lambda b, hp: (b, hp)),
        out_shape=jax.ShapeDtypeStruct((m, D_C), BF16),
        scratch_shapes=[pltpu.VMEM((nblk, MOBA_BLOCK, LANES), BF16),
                        pltpu.VMEM((nblk, LANES + ONES_ROWS, MOBA_BLOCK), BF16)],
        compiler_params=_cparams(("parallel", "parallel")),
        name="moba_prompt",
    )(q_rot, k_rot, v_att, bias)


SAMPLE_PAGE_BUFFERS = 32
SAMPLE_PAGE_UNROLL = 8


def _moba_sample_kernel(pt_ref, q_ref, kn_ref, vn_ref, ck_hbm, cv_hbm, o_ref,
                        buf_ref, sem, s_ref, own_ref, oacc_ref, *, n_seq, n_pages, t_len, layer):
    nbuf = SAMPLE_PAGE_BUFFERS
    per_seq = 2 * n_pages
    total = n_seq * per_seq
    scale = ATT_HEAD_DIM ** -0.5
    ppb = MOBA_BLOCK // PAGE_SIZE
    n_blk = n_pages // ppb
    n_row = ATT_HEADS * t_len
    rows = lambda h: slice(h * t_len, (h + 1) * t_len)

    def page_copy(src_hbm, page, slot):
        return pltpu.make_async_copy(src_hbm.at[page, layer], buf_ref.at[slot], sem.at[slot])

    def start_fetch(n, slot):
        b = n // per_seq
        i = n % per_seq

        @pl.when(i < n_pages)
        def _():
            page_copy(ck_hbm, pt_ref[b, i], slot).start()

        @pl.when(i >= n_pages)
        def _():
            page_copy(cv_hbm, pt_ref[b, i - n_pages], slot).start()

    def finish_fetch(slot):
        page_copy(ck_hbm, 0, slot).wait()

    def refill(n, slot):
        @pl.when(n + nbuf < total)
        def _():
            start_fetch(n + nbuf, slot)

    b = pl.program_id(0)

    @pl.when(b == 0)
    def _():
        for n in range(nbuf):
            start_fetch(n, n)

    def one_sequence():
        base = b * per_seq

        grp = SAMPLE_PAGE_UNROLL

        def score_pages(g, c):
            pages = [g * grp + u for u in range(grp)]
            for i in pages:
                finish_fetch(i % nbuf)
            for i in pages:
                for h in range(ATT_HEADS):
                    sc = _dot(q_ref[h].astype(BF16), buf_ref[i % nbuf, h].astype(BF16))
                    s_ref[i, rows(h), :] = sc * scale
            for i in pages:
                refill(base + i, i % nbuf)
            return c

        lax.fori_loop(0, n_pages // grp, score_pages, 0)
        select_and_normalise(q_ref, kn_ref, vn_ref)

        def weigh_pages(g, accs):
            pages = [g * grp + u for u in range(grp)]
            for i in pages:
                finish_fetch((n_pages + i) % nbuf)
            accs = tuple(
                accs[h] + _dot_nt(
                    jnp.concatenate([s_ref[i, rows(h), :] for i in pages], axis=1).astype(BF16),
                    jnp.concatenate([buf_ref[(n_pages + i) % nbuf, h] for i in pages], axis=1).astype(BF16))
                for h in range(ATT_HEADS))
            for i in pages:
                refill(base + n_pages + i, (n_pages + i) % nbuf)
            return accs

        accs = lax.fori_loop(0, n_pages // grp, weigh_pages, tuple(oacc_ref[h] for h in range(ATT_HEADS)))
        for h in range(ATT_HEADS):
            o_ref[h] = accs[h] * own_ref[rows(h), 0:1]

    def select_and_normalise(q_ref, kn_ref, vn_ref):
        lane = lax.broadcasted_iota(jnp.int32, (n_row, LANES), 1)
        gate = jnp.full((n_row, LANES), NEG_INF, F32)
        for j in range(n_blk):
            tot = s_ref[ppb * j]
            for i in range(1, ppb):
                tot = tot + s_ref[ppb * j + i]
            gate = jnp.where(lane == j, jnp.sum(tot, axis=1, keepdims=True), gate)
        sel = jnp.where(_topk_rank(gate, lane, n_blk, 1) < min(MOBA_TOPK, n_blk), 1, 0)
        zpad = jnp.zeros((PAGE_SIZE - t_len, ATT_HEAD_DIM), F32)
        for h in range(ATT_HEADS):
            kh = jnp.concatenate([kn_ref[:, h, :], zpad], axis=0).astype(BF16)
            own_ref[rows(h), :] = _dot_nt(q_ref[h].astype(BF16), kh) * scale
        tok = lax.broadcasted_iota(jnp.int32, (n_row, LANES), 0) % t_len
        so = jnp.where(lane <= tok, own_ref[...], NEG_INF)
        mv = so
        for j in range(n_blk):
            for i in range(ppb):
                mv = jnp.maximum(mv, jnp.where(sel[:, j:j + 1] > 0, s_ref[ppb * j + i], NEG_INF))
        m = jnp.max(mv, axis=1, keepdims=True)
        po = jnp.exp(so - m)
        own_ref[...] = po
        lv = po
        for j in range(n_blk):
            for i in range(ppb):
                pj = jnp.exp(jnp.where(sel[:, j:j + 1] > 0, s_ref[ppb * j + i], NEG_INF) - m)
                s_ref[ppb * j + i] = pj
                lv = lv + pj
        inv = 1.0 / jnp.sum(lv, axis=1, keepdims=True)
        for h in range(ATT_HEADS):
            vh = jnp.concatenate([vn_ref[:, h, :], zpad], axis=0).astype(BF16)
            oacc_ref[h] = _dot(own_ref[rows(h), :].astype(BF16), vh)
        own_ref[:, 0:1] = inv

    one_sequence()


def _moba_sample(q_htd, k_thd, v_thd, cache_kt, cache_vt, page_table, layer):
    n_seq, _, t_len, _ = q_htd.shape
    n_pages = page_table.shape[1]
    assert (2 * n_pages) % SAMPLE_PAGE_BUFFERS == 0 and n_pages % SAMPLE_PAGE_UNROLL == 0
    tok_shape = (n_seq, ATT_HEADS, t_len, ATT_HEAD_DIM)
    per_seq = lambda shape: pl.BlockSpec((None,) + shape, lambda b, pt: (b, 0, 0, 0))
    tok_blk = per_seq((ATT_HEADS, t_len, ATT_HEAD_DIM))
    new_blk = per_seq((t_len, ATT_HEADS, ATT_HEAD_DIM))
    grid_spec = pltpu.PrefetchScalarGridSpec(
        num_scalar_prefetch=1,
        grid=(n_seq,),
        in_specs=[tok_blk, new_blk, new_blk, pl.BlockSpec(memory_space=pl.ANY), pl.BlockSpec(memory_space=pl.ANY)],
        out_specs=tok_blk,
        scratch_shapes=[pltpu.VMEM((SAMPLE_PAGE_BUFFERS, ATT_HEADS, ATT_HEAD_DIM, PAGE_SIZE), F32),
                        pltpu.SemaphoreType.DMA((SAMPLE_PAGE_BUFFERS,)),
                        pltpu.VMEM((n_pages, ATT_HEADS * t_len, PAGE_SIZE), F32),
                        pltpu.VMEM((ATT_HEADS * t_len, PAGE_SIZE), F32),
                        pltpu.VMEM((ATT_HEADS, t_len, ATT_HEAD_DIM), F32)])
    return pl.pallas_call(
        functools.partial(_moba_sample_kernel, n_seq=n_seq, n_pages=n_pages, t_len=t_len, layer=layer),
        grid_spec=grid_spec,
        out_shape=jax.ShapeDtypeStruct(tok_shape, F32),
        compiler_params=_cparams(("arbitrary",)),
        name="moba_sample",
    )(page_table, q_htd, k_thd, v_thd, cache_kt, cache_vt)


def _merge_kernel(x_ref, ya_ref, yb_ref, yc_ref, g0_ref, g1_ref, g2_ref, bg_ref, wa_ref, wb_ref, wc_ref, wo_ref, o_ref):
    bg = bg_ref[...]
    merged = (jax.nn.sigmoid(g0_ref[...] + bg[:, :D_MODEL]) * _dot(ya_ref[...], wa_ref[...])
              + jax.nn.sigmoid(g1_ref[...] + bg[:, D_MODEL:2 * D_MODEL]) * _dot(yb_ref[...], wb_ref[...])
              + jax.nn.sigmoid(g2_ref[...] + bg[:, 2 * D_MODEL:]) * _dot(yc_ref[...], wc_ref[...]))
    o_ref[...] = x_ref[...] + _dot(merged.astype(BF16), wo_ref[...])


def _merge(x2d, ya, yb, yc, proj, b_gate, wa, wb, wc, wo, tm):
    m = x2d.shape[0]
    const = lambda i: (0, 0)
    gate_spec = lambda k: pl.BlockSpec((tm, D_MODEL), lambda i: (i, C_GATE // D_MODEL + k))
    return pl.pallas_call(
        _merge_kernel,
        grid=(m // tm,),
        in_specs=[pl.BlockSpec((tm, D_MODEL), lambda i: (i, 0)),
                  pl.BlockSpec((tm, D_A), lambda i: (i, 0)),
                  pl.BlockSpec((tm, D_B), lambda i: (i, 0)),
                  pl.BlockSpec((tm, D_C), lambda i: (i, 0)),
                  gate_spec(0), gate_spec(1), gate_spec(2),
                  pl.BlockSpec((1, N_BRANCH * D_MODEL), const),
                  pl.BlockSpec((D_A, D_MODEL), const),
                  pl.BlockSpec((D_B, D_MODEL), const),
                  pl.BlockSpec((D_C, D_MODEL), const),
                  pl.BlockSpec((D_MODEL, D_MODEL), const)],
        out_specs=pl.BlockSpec((tm, D_MODEL), lambda i: (i, 0)),
        out_shape=jax.ShapeDtypeStruct((m, D_MODEL), F32),
        compiler_params=_cparams(("parallel",)),
        name="merge",
    )(x2d, ya, yb, yc, proj, proj, proj, b_gate, wa, wb, wc, wo)


def _ffn_kernel(x_ref, g_ref, wu_ref, wd_ref, gf_ref, o_ref, h_ref, acc_ref, *, final_norm):
    j = pl.program_id(1)

    @pl.when(j == 0)
    def _():
        h_ref[...] = _rms(x_ref[...], g_ref[...]).astype(BF16)
        acc_ref[...] = x_ref[...]

    f = jnp.maximum(_dot(h_ref[...], wu_ref[...]), 0.0)
    acc_ref[...] += _dot((f * f).astype(BF16), wd_ref[...])

    @pl.when(j == pl.num_programs(1) - 1)
    def _():
        y = acc_ref[...]
        o_ref[...] = _rms(y, gf_ref[...]) if final_norm else y


def _ffn(x2d, g, wu, wd, gf, tm, tf, final_norm):
    m = x2d.shape[0]
    return pl.pallas_call(
        functools.partial(_ffn_kernel, final_norm=final_norm),
        grid=(m // tm, D_FF // tf),
        in_specs=[pl.BlockSpec((tm, D_MODEL), lambda i, j: (i, 0)),
                  pl.BlockSpec((1, D_MODEL), lambda i, j: (0, 0)),
                  pl.BlockSpec((D_MODEL, tf), lambda i, j: (0, j)),
                  pl.BlockSpec((tf, D_MODEL), lambda i, j: (j, 0)),
                  pl.BlockSpec((1, D_MODEL), lambda i, j: (0, 0))],
        out_specs=pl.BlockSpec((tm, D_MODEL), lambda i, j: (i, 0)),
        out_shape=jax.ShapeDtypeStruct((m, D_MODEL), F32),
        scratch_shapes=[pltpu.VMEM((tm, D_MODEL), BF16), pltpu.VMEM((tm, D_MODEL), F32)],
        compiler_params=_cparams(("parallel", "arbitrary")),
        name="ffn",
    )(x2d, g, wu, wd, gf)


def _prep_layer(l, norm1_g, w_in, b_gate, ln_v_g, ln_v_b, w_spatial, b_spatial, w_a_out, conv_w, conv_b,
                dt_bias, a_log, d_skip, ssm_norm_g, w_b_out, w_c_out, w_o, norm2_g, w_up, w_down, t_len):
    w = w_in[l]
    o_dt = 2 * D_A + D_B + CONV_DIM
    o_q = o_dt + SSM_HEADS
    o_gate = o_q + 3 * D_C
    w_perm = jnp.concatenate(
        [w[:, :o_dt], w[:, o_gate:], w[:, o_q:o_gate], w[:, o_dt:o_q],
         jnp.zeros((D_MODEL, DT_W - SSM_HEADS), w.dtype)], axis=1).astype(BF16)
    tril = jnp.tril(jnp.ones((A_CHUNK, A_CHUNK), bool))
    w_tril = jnp.where(tril[None], w_spatial[l], 0)
    bs_exp = jnp.repeat(b_spatial[l].T, A_GROUP_DIM, axis=1)
    w_exp = jnp.repeat(jnp.transpose(w_tril[:, :t_len, :t_len], (2, 1, 0)), A_GROUP_DIM, axis=2)
    pad_h = lambda v: jnp.concatenate([v, jnp.zeros((LANES - SSM_HEADS,), v.dtype)])[None, :]
    return dict(
        norm1_g=norm1_g[l][None, :], w_in=w_perm, b_gate=b_gate[l][None, :],
        ln_g=ln_v_g[l][None, :], ln_b=ln_v_b[l][None, :],
        w_tril=w_tril.astype(BF16), bs_exp=bs_exp, w_exp=w_exp, bs_exp_s=bs_exp[:t_len],
        conv_w=conv_w[l], conv_b=conv_b[l][None, :], dt_bias=pad_h(dt_bias[l]), a_log=pad_h(a_log[l]),
        d_skip=jnp.repeat(d_skip[l], SSM_HEAD_DIM)[None, :], ssm_norm_g=ssm_norm_g[l][None, :],
        w_a_out=w_a_out[l].astype(BF16), w_b_out=w_b_out[l].astype(BF16), w_c_out=w_c_out[l].astype(BF16),
        w_o=w_o[l].astype(BF16), norm2_g=norm2_g[l][None, :],
        w_up=w_up[l].astype(BF16), w_down=w_down[l].astype(BF16))


def _rope_tables(pos):
    inv = jnp.power(jnp.float32(ROPE_THETA), -jnp.arange(0, ATT_HEAD_DIM, 2, dtype=F32) / ATT_HEAD_DIM)
    ang = pos.astype(F32)[:, None] * inv[None, :]
    cos, sin = jnp.cos(ang), jnp.sin(ang)
    cos_h = jnp.concatenate([cos, cos], axis=1)
    sin_h = jnp.concatenate([-sin, sin], axis=1)
    return jnp.tile(cos_h, (1, ATT_HEADS)), jnp.tile(sin_h, (1, ATT_HEADS))


def kernel(x_prompt, x_sample, cache_k, cache_v, state_ssm, state_conv, page_table, norm1_g, w_in, b_gate, ln_v_g, ln_v_b, w_spatial, b_spatial, w_a_out, conv_w, conv_b, dt_bias, a_log, d_skip, ssm_norm_g, w_b_out, w_c_out, w_o, norm2_g, w_up, w_down, norm_f_g):
    bp, s_len, _ = x_prompt.shape
    db, t_len, _ = x_sample.shape
    depth = w_in.shape[0]
    past_len = page_table.shape[1] * PAGE_SIZE
    mp, ms = bp * s_len, db * t_len
    cos_p, sin_p = _rope_tables(jnp.arange(s_len))
    cos_s, sin_s = _rope_tables(past_len + jnp.arange(t_len))
    cos_s, sin_s = jnp.tile(cos_s, (db, 1)), jnp.tile(sin_s, (db, 1))
    conv_zero = jnp.zeros((1, bp, SSM_CONV - 1, CONV_DIM), F32)
    h_zero = jnp.zeros((1, bp, D_B, SSM_STATE), F32)
    gf = norm_f_g[None, :]
    tm_p = 1024
    cache_kt = jnp.transpose(cache_k, (0, 1, 3, 4, 2))
    cache_vt = jnp.transpose(cache_v, (0, 1, 3, 4, 2))

    xp = x_prompt.reshape(mp, D_MODEL)
    xs = x_sample.reshape(ms, D_MODEL)
    outs = {k: [] for k in ("a_v_s", "ssm_p", "ssm_s", "conv_p", "conv_s", "k_p", "v_p", "k_s", "v_s")}
    for l in range(depth):
        w = _prep_layer(l, norm1_g, w_in, b_gate, ln_v_g, ln_v_b, w_spatial, b_spatial, w_a_out, conv_w, conv_b,
                        dt_bias, a_log, d_skip, ssm_norm_g, w_b_out, w_c_out, w_o, norm2_g, w_up, w_down, t_len)
        last = l == depth - 1

        proj, v_att = _in_proj(xp, w["norm1_g"], w["w_in"], tm_p)
        ya = _branch_a_prompt(proj, w["ln_g"], w["ln_b"], w["w_tril"], w["bs_exp"])
        yb, h_new = _branch_b(proj, bp, SSM_CHUNK, conv_zero, h_zero, 0, w["conv_w"], w["conv_b"], w["dt_bias"],
                              w["a_log"], w["d_skip"], w["ssm_norm_g"])
        q_rot, k_rot, sel_bias = _rope_prompt(proj, bp, cos_p, sin_p)
        yc = _moba_prompt(q_rot, k_rot, v_att, sel_bias, bp)
        x1 = _merge(xp, ya, yb, yc, proj, w["b_gate"], w["w_a_out"], w["w_b_out"], w["w_c_out"], w["w_o"], 512)
        xp = _ffn(x1, w["norm2_g"], w["w_up"], w["w_down"], gf, tm_p, 1024, last)
        proj3 = proj.reshape(bp, s_len, N_COLS)
        outs["ssm_p"].append(h_new.reshape(bp, SSM_HEADS, SSM_HEAD_DIM, SSM_STATE))
        outs["conv_p"].append(proj3[:, s_len - (SSM_CONV - 1):, C_XBC:C_XBC + CONV_DIM])
        outs["k_p"].append(k_rot.reshape(bp, s_len, ATT_HEADS, ATT_HEAD_DIM))
        outs["v_p"].append(v_att.reshape(bp, s_len, ATT_HEADS, ATT_HEAD_DIM))

        proj, v_att = _in_proj(xs, w["norm1_g"], w["w_in"], ms)
        ya, vn = _branch_a_sample(proj, t_len, w["ln_g"], w["ln_b"], w["w_exp"], w["bs_exp_s"])
        yb, h_new = _branch_b(proj, db, t_len, state_conv, state_ssm.reshape(depth, db, D_B, SSM_STATE), l,
                              w["conv_w"], w["conv_b"], w["dt_bias"], w["a_log"], w["d_skip"], w["ssm_norm_g"])
        q_rot, k_rot = _rope_sample(proj, cos_s, sin_s)
        proj3 = proj.reshape(db, t_len, N_COLS)
        v_new = v_att.reshape(db, t_len, ATT_HEADS, ATT_HEAD_DIM)
        k_new = k_rot.reshape(db, t_len, ATT_HEADS, ATT_HEAD_DIM)
        q_htd = jnp.transpose(q_rot.reshape(db, t_len, ATT_HEADS, ATT_HEAD_DIM), (0, 2, 1, 3))
        o_htd = _moba_sample(q_htd, k_new, v_new, cache_kt, cache_vt, page_table, l)
        yc = jnp.transpose(o_htd, (0, 2, 1, 3)).reshape(ms, D_C).astype(BF16)
        x1 = _merge(xs, ya, yb, yc, proj, w["b_gate"], w["w_a_out"], w["w_b_out"], w["w_c_out"], w["w_o"], ms)
        xs = _ffn(x1, w["norm2_g"], w["w_up"], w["w_down"], gf, ms, 1024, last)
        outs["a_v_s"].append(vn.reshape(db, t_len, D_A))
        outs["ssm_s"].append(h_new.reshape(db, SSM_HEADS, SSM_HEAD_DIM, SSM_STATE))
        if t_len >= SSM_CONV - 1:
            conv_new = proj3[:, t_len - (SSM_CONV - 1):, C_XBC:C_XBC + CONV_DIM]
        else:
            conv_new = jnp.concatenate([state_conv[l], proj3[:, :, C_XBC:C_XBC + CONV_DIM]],
                                       axis=1)[:, -(SSM_CONV - 1):]
        outs["conv_s"].append(conv_new)
        outs["k_s"].append(k_new)
        outs["v_s"].append(v_new)

    st = lambda k: jnp.stack(outs[k])
    return (xp.reshape(bp, s_len, D_MODEL), xs.reshape(db, t_len, D_MODEL), st("a_v_s"), st("ssm_p"), st("ssm_s"),
            st("conv_p"), st("conv_s"), st("k_p"), st("v_p"), st("k_s"), st("v_s"))
```
